```python
import jax, jax.numpy as jnp
from jax import lax
import numpy as np

D_MODEL = 4096
BATCH = 2
SEQ = 8192
DEPTH = 2

D_MIX = D_MODEL
D_GROUP = D_MIX // 4
N_GROUP_HEADS = 8
HEAD_DIM = D_GROUP // N_GROUP_HEADS
D_CONV = D_GROUP
CONV_WIDTH = 31
D_FOX = D_GROUP
FOX_BLOCK = 128
RET_V_DIM = HEAD_DIM
RET_K_DIM = HEAD_DIM // 2
D_RET_QK = N_GROUP_HEADS * RET_K_DIM
D_RET_V = D_GROUP
RET_CHUNK = 128
D_SG = D_GROUP
SG_CHUNK = 128
D_FF = 4 * D_MODEL
ROPE_BASE = 10000.0
EPS = 1e-6
GN_EPS = 1e-5
IN_SIZES = (D_CONV, D_CONV, D_FOX, D_FOX, D_FOX, N_GROUP_HEADS, D_RET_QK, D_RET_QK, D_RET_V, D_RET_V, 2 * D_SG)
D_IN = 2 * D_CONV + 3 * D_FOX + N_GROUP_HEADS + 2 * D_RET_QK + 2 * D_RET_V + 2 * D_SG

kernel_name = 'hybrid_conv_fox_retention_sgu_block'


def _rms_norm(x, g):
    xf = x.astype(jnp.float32)
    y = xf * lax.rsqrt(jnp.mean(xf * xf, axis=-1, keepdims=True) + EPS)
    return (y * g.astype(jnp.float32)).astype(x.dtype)


def _layer_norm(x, g, b):
    xf = x.astype(jnp.float32)
    mu = jnp.mean(xf, axis=-1, keepdims=True)
    var = jnp.mean(jnp.square(xf - mu), axis=-1, keepdims=True)
    y = (xf - mu) * lax.rsqrt(var + EPS)
    return (y * g.astype(jnp.float32) + b.astype(jnp.float32)).astype(x.dtype)


def _conv_module(a, gate, conv_w, conv_b, ln_g, ln_b):
    h = a * jax.nn.sigmoid(gate)
    h = lax.conv_general_dilated(h, conv_w[:, None, :], window_strides=(1,),
                                 padding=[(CONV_WIDTH - 1, 0)],
                                 dimension_numbers=('NWC', 'WIO', 'NWC'),
                                 feature_group_count=D_CONV) + conv_b
    return jax.nn.silu(_layer_norm(h, ln_g, ln_b))


def _forgetting_attention(q, k, v, f_logit, f_bias):
    bsz, seq = q.shape[0], q.shape[1]
    h, d = N_GROUP_HEADS, HEAD_DIM
    q = q.reshape(bsz, seq, h, d).transpose(0, 2, 1, 3)
    k = k.reshape(bsz, seq, h, d).transpose(0, 2, 1, 3)
    v = v.reshape(bsz, seq, h, d).transpose(0, 2, 1, 3)
    log_f = jax.nn.log_sigmoid((f_logit + f_bias).astype(jnp.float32))
    cum = jnp.cumsum(log_f, axis=1).transpose(0, 2, 1)
    nb = seq // FOX_BLOCK
    q_blocks = q.reshape(bsz, h, nb, FOX_BLOCK, d).transpose(2, 0, 1, 3, 4)
    c_blocks = cum.reshape(bsz, h, nb, FOX_BLOCK).transpose(2, 0, 1, 3)
    key_pos = jnp.arange(seq)
    scale = d ** -0.5

    def block(args):
        qi, ci, i = args
        s = jnp.einsum('bhqd,bhkd->bhqk', qi, k).astype(jnp.float32) * scale
        s = s + ci[..., :, None] - cum[..., None, :]
        q_pos = i * FOX_BLOCK + jnp.arange(FOX_BLOCK)
        s = jnp.where(key_pos[None, :] <= q_pos[:, None], s, -jnp.inf)
        p = jax.nn.softmax(s, axis=-1).astype(v.dtype)
        return jnp.einsum('bhqk,bhkd->bhqd', p, v)

    out = lax.map(block, (q_blocks, c_blocks, jnp.arange(nb)))
    return out.transpose(1, 0, 3, 2, 4).reshape(bsz, seq, h * d)


def _rotary(x, pos):
    half = x.shape[-1] // 2
    inv = 1.0 / (ROPE_BASE ** jnp.linspace(0.0, 1.0, half, dtype=jnp.float32))
    ang = pos.astype(jnp.float32)[:, None] * inv[None, :]
    cos = jnp.cos(ang)[None, :, None, :]
    sin = jnp.sin(ang)[None, :, None, :]
    x1 = x[..., :half].astype(jnp.float32)
    x2 = x[..., half:].astype(jnp.float32)
    return jnp.concatenate([x1 * cos - x2 * sin, x2 * cos + x1 * sin], axis=-1).astype(x.dtype)


def _retention(q, k, v, g):
    bsz, seq = q.shape[0], q.shape[1]
    h, dk, dv, C = N_GROUP_HEADS, RET_K_DIM, RET_V_DIM, RET_CHUNK
    nc = seq // C
    pos = jnp.arange(seq)
    q = _rotary(q.reshape(bsz, seq, h, dk), pos)
    k = _rotary(k.reshape(bsz, seq, h, dk), pos) * (dk ** -0.5)
    log_gamma = jnp.log(1.0 - 2.0 ** (-5.0 - jnp.arange(h, dtype=jnp.float32)))

    def chunks(t, d):
        return t.reshape(bsz, nc, C, h, d).transpose(0, 3, 1, 2, 4)

    qc, kc, vc = chunks(q, dk), chunks(k, dk), chunks(v, dv)
    idx = jnp.arange(C, dtype=jnp.float32)
    rel = idx[:, None] - idx[None, :]
    decay = jnp.where(rel >= 0, jnp.exp(log_gamma[:, None, None] * jnp.maximum(rel, 0.0)), 0.0)
    scores = jnp.einsum('bhnid,bhnjd->bhnij', qc, kc) * decay[None, :, None]
    inner = jnp.einsum('bhnij,bhnje->bhnie', scores, vc)
    k_w = jnp.exp(log_gamma[:, None] * (C - 1.0 - idx)[None, :])
    kv = jnp.einsum('bhnjd,bhnje->bhnde', kc * k_w[None, :, None, :, None], vc)
    chunk_decay = jnp.exp(log_gamma * C)[None, :, None, None]

    def step(state, kv_n):
        return state * chunk_decay + kv_n, state

    _, states = lax.scan(step, jnp.zeros((bsz, h, dk, dv), kv.dtype), kv.transpose(2, 0, 1, 3, 4))
    states = states.transpose(1, 2, 0, 3, 4)
    q_w = jnp.exp(log_gamma[:, None] * (idx + 1.0)[None, :])
    cross = jnp.einsum('bhnid,bhnde->bhnie', qc * q_w[None, :, None, :, None], states)
    y = (inner + cross).astype(jnp.float32)
    mu = jnp.mean(y, axis=-1, keepdims=True)
    var = jnp.mean(jnp.square(y - mu), axis=-1, keepdims=True)
    y = (y - mu) * lax.rsqrt(var + GN_EPS)
    y = y.transpose(0, 2, 3, 1, 4).reshape(bsz, seq, h * dv)
    return (jax.nn.silu(g.astype(jnp.float32)) * y).astype(g.dtype)


def _spatial_gating(z, w_s, b_s, ln_g, ln_b):
    bsz, seq = z.shape[0], z.shape[1]
    nc = seq // SG_CHUNK
    z = jax.nn.gelu(z)
    u, v = jnp.split(z, 2, axis=-1)
    v = _layer_norm(v, ln_g, ln_b).reshape(bsz, nc, SG_CHUNK, N_GROUP_HEADS, HEAD_DIM)
    causal = jnp.tril(jnp.ones((SG_CHUNK, SG_CHUNK), w_s.dtype))
    mixed = jnp.einsum('gts,bnsgc->bntgc', w_s * causal, v) + b_s.T[None, None, :, :, None]
    return u * mixed.reshape(bsz, seq, D_SG)


def _hybrid_layer(x, c, ada_w, ada_b, mix_pre_g, mix_post_g, w_in, fox_f_bias, conv_w, conv_b,
                  conv_ln_g, conv_ln_b, sg_w, sg_b, sg_ln_g, sg_ln_b, w_out, ffn_pre_g, ffn_post_g,
                  w_ff1, w_ff2):
    mod = jax.nn.silu(c) @ ada_w + ada_b
    sh1, sc1, gt1, sh2, sc2, gt2 = [m[:, None, :] for m in jnp.split(mod, 6, axis=-1)]
    h = _rms_norm(x, mix_pre_g) * (1.0 + sc1) + sh1
    z = h @ w_in
    offsets = np.cumsum(IN_SIZES)[:-1].tolist()
    glu_a, glu_b, fq, fk, fv, ff, rq, rk, rv, rg, sgz = jnp.split(z, offsets, axis=-1)
    y_conv = _conv_module(glu_a, glu_b, conv_w, conv_b, conv_ln_g, conv_ln_b)
    y_fox = _forgetting_attention(fq, fk, fv, ff, fox_f_bias)
    y_ret = _retention(rq, rk, rv, rg)
    y_sg = _spatial_gating(sgz, sg_w, sg_b, sg_ln_g, sg_ln_b)
    y = jnp.concatenate([y_conv, y_fox, y_ret, y_sg], axis=-1) @ w_out
    x = x + gt1 * _rms_norm(y, mix_post_g)
    h = _rms_norm(x, ffn_pre_g) * (1.0 + sc2) + sh2
    y = jnp.square(jax.nn.relu(h @ w_ff1)) @ w_ff2
    return x + gt2 * _rms_norm(y, ffn_post_g)


def setup_inputs(seed: int = 0) -> dict:
    key = jax.random.key(seed)
    ks = jax.random.split(key, 21)
    L = DEPTH

    def nrm(k, shape, s):
        return jax.random.normal(k, shape, jnp.float32) * s

    return {
        'x': nrm(ks[0], (BATCH, SEQ, D_MODEL), 1.0),
        'c': nrm(ks[1], (BATCH, D_MODEL), 1.0),
        'ada_w': nrm(ks[2], (L, D_MODEL, 6 * D_MODEL), 0.5 * D_MODEL ** -0.5),
        'ada_b': nrm(ks[3], (L, 6 * D_MODEL), 0.01),
        'mix_pre_g': 1.0 + nrm(ks[4], (L, D_MODEL), 0.02),
        'mix_post_g': 1.0 + nrm(ks[5], (L, D_MODEL), 0.02),
        'w_in': nrm(ks[6], (L, D_MODEL, D_IN), D_MODEL ** -0.5),
        'fox_f_bias': 2.0 + nrm(ks[7], (L, N_GROUP_HEADS), 0.5),
        'conv_w': nrm(ks[8], (L, CONV_WIDTH, D_CONV), CONV_WIDTH ** -0.5),
        'conv_b': nrm(ks[9], (L, D_CONV), 0.02),
        'conv_ln_g': 1.0 + nrm(ks[10], (L, D_CONV), 0.02),
        'conv_ln_b': nrm(ks[11], (L, D_CONV), 0.02),
        'sg_w': nrm(ks[12], (L, N_GROUP_HEADS, SG_CHUNK, SG_CHUNK), SG_CHUNK ** -0.5),
        'sg_b': 1.0 + nrm(ks[13], (L, N_GROUP_HEADS, SG_CHUNK), 0.02),
        'sg_ln_g': 1.0 + nrm(ks[14], (L, D_SG), 0.02),
        'sg_ln_b': nrm(ks[15], (L, D_SG), 0.02),
        'w_out': nrm(ks[16], (L, D_MIX, D_MODEL), D_MIX ** -0.5),
        'ffn_pre_g': 1.0 + nrm(ks[17], (L, D_MODEL), 0.02),
        'ffn_post_g': 1.0 + nrm(ks[18], (L, D_MODEL), 0.02),
        'w_ff1': nrm(ks[19], (L, D_MODEL, D_FF), D_MODEL ** -0.5),
        'w_ff2': nrm(ks[20], (L, D_FF, D_MODEL), D_FF ** -0.5),
    }


def reference(x, c, ada_w, ada_b, mix_pre_g, mix_post_g, w_in, fox_f_bias, conv_w, conv_b,
              conv_ln_g, conv_ln_b, sg_w, sg_b, sg_ln_g, sg_ln_b, w_out, ffn_pre_g, ffn_post_g,
              w_ff1, w_ff2):
    for l in range(DEPTH):
        x = _hybrid_layer(x, c, ada_w[l], ada_b[l], mix_pre_g[l], mix_post_g[l], w_in[l],
                          fox_f_bias[l], conv_w[l], conv_b[l], conv_ln_g[l], conv_ln_b[l],
                          sg_w[l], sg_b[l], sg_ln_g[l], sg_ln_b[l], w_out[l], ffn_pre_g[l],
                          ffn_post_g[l], w_ff1[l], w_ff2[l])
    return x
```

```python
import functools
import math

import jax
import jax.numpy as jnp
from jax import lax
from jax.experimental import pallas as pl
from jax.experimental.pallas import tpu as pltpu

F32 = jnp.float32
BF16 = jnp.bfloat16

D_GROUP = 1024
N_HEADS = 8
HEAD_DIM = D_GROUP // N_HEADS
CONV_WIDTH = 31
CONV_HALO = 32
RET_K_DIM = HEAD_DIM // 2
SG_CHUNK = 128
ROPE_BASE = 10000.0
EPS = 1e-6
GN_EPS = 1e-5

Z_GLU_A, Z_GLU_B = 0, 1024
Z_FQ, Z_FK, Z_FV = 2048, 3072, 4096
Z_RQ, Z_RK, Z_RV, Z_RG = 5120, 5632, 6144, 7168
Z_SGU, Z_SGV = 8192, 9216
D_Z = 10240
FF_OFF = 5120

MIB = 1024 * 1024
LANES = 128


def _cparams(semantics, vmem_mib):
    return pltpu.CompilerParams(dimension_semantics=semantics, vmem_limit_bytes=vmem_mib * MIB)


def _mod_kernel(c_ref, w_ref, b_ref, o_ref):
    c = c_ref[...]
    s = (c * jax.nn.sigmoid(c)).astype(BF16)
    w = w_ref[0].astype(BF16)
    o_ref[0] = jnp.dot(s, w, preferred_element_type=F32) + b_ref[0]


def _modulation(c, ada_w, ada_b):
    n_layers, d, n = ada_w.shape
    bsz = c.shape[0]
    rows = 8
    c_pad = jnp.zeros((rows, d), F32).at[:bsz].set(c)
    tn = 512
    out = pl.pallas_call(
        _mod_kernel,
        grid=(n_layers, n // tn),
        in_specs=[
            pl.BlockSpec((rows, d), lambda l, j: (0, 0)),
            pl.BlockSpec((1, d, tn), lambda l, j: (l, 0, j)),
            pl.BlockSpec((1, 1, tn), lambda l, j: (l, 0, j)),
        ],
        out_specs=pl.BlockSpec((1, rows, tn), lambda l, j: (l, 0, j)),
        out_shape=jax.ShapeDtypeStruct((n_layers, rows, n), F32),
        compiler_params=_cparams(("arbitrary", "arbitrary"), 40),
    )(c_pad, ada_w, ada_b.reshape(n_layers, 1, n))
    return out[:, :bsz]


def _norm_mod(x, g, scale, shift):
    r = lax.rsqrt(jnp.mean(x * x, axis=-1, keepdims=True) + EPS)
    return x * r * (g * (1.0 + scale)) + shift


def _prenorm_kernel(x_ref, g_ref, sc_ref, sh_ref, o_ref):
    o_ref[...] = _norm_mod(x_ref[...], g_ref[...], sc_ref[0], sh_ref[0]).astype(BF16)


def _log_sigmoid(x):
    return jnp.minimum(x, 0.0) - jnp.log(1.0 + jnp.exp(-jnp.abs(x)))


def _prenorm_forget_kernel(x_ref, g_ref, sc_ref, sh_ref, wf_ref, fb_ref, tri_ref, o_ref, cum_ref,
                           carry_ref, *, tiles_per_seq):
    i = pl.program_id(0)
    hb = _norm_mod(x_ref[...], g_ref[...], sc_ref[0], sh_ref[0]).astype(BF16)
    o_ref[...] = hb
    logit = lax.dot_general(wf_ref[...], hb, (((1,), (1,)), ((), ())), preferred_element_type=F32)
    logf = _log_sigmoid(logit + fb_ref[...])
    p1 = logf.astype(BF16)
    r1 = logf - p1.astype(F32)
    p2 = r1.astype(BF16)
    p3 = (r1 - p2.astype(F32)).astype(BF16)
    tri = tri_ref[...]
    local = (jnp.dot(p1, tri, preferred_element_type=F32) + jnp.dot(p2, tri, preferred_element_type=F32)
             + jnp.dot(p3, tri, preferred_element_type=F32))

    @pl.when(i % tiles_per_seq == 0)
    def _():
        carry_ref[...] = jnp.zeros_like(carry_ref)

    cum = carry_ref[:, 0:1] + local
    cum_ref[0] = cum
    bm = cum.shape[1]
    carry_ref[...] = jnp.broadcast_to(cum[:, bm - 1:bm], carry_ref.shape)


def _prenorm(x2, g, scale, shift, seq):
    m, d = x2.shape
    bm = min(256, seq)
    per_seq = seq // bm
    return pl.pallas_call(
        _prenorm_kernel,
        grid=(m // bm,),
        in_specs=[
            pl.BlockSpec((bm, d), lambda i: (i, 0)),
            pl.BlockSpec((1, d), lambda i: (0, 0)),
            pl.BlockSpec((1, 1, d), lambda i: (i // per_seq, 0, 0)),
            pl.BlockSpec((1, 1, d), lambda i: (i // per_seq, 0, 0)),
        ],
        out_specs=pl.BlockSpec((bm, d), lambda i: (i, 0)),
        out_shape=jax.ShapeDtypeStruct((m, d), BF16),
        compiler_params=_cparams(("arbitrary",), 40),
    )(x2, g.reshape(1, d), scale, shift)


def _prenorm_forget(x2, g, scale, shift, wf_t, f_bias, seq):
    m, d = x2.shape
    bsz = m // seq
    bm = min(256, seq)
    per_seq = seq // bm
    tri = (lax.broadcasted_iota(jnp.int32, (bm, bm), 0) <= lax.broadcasted_iota(jnp.int32, (bm, bm), 1)).astype(BF16)
    return pl.pallas_call(
        functools.partial(_prenorm_forget_kernel, tiles_per_seq=per_seq),
        grid=(m // bm,),
        in_specs=[
            pl.BlockSpec((bm, d), lambda i: (i, 0)),
            pl.BlockSpec((1, d), lambda i: (0, 0)),
            pl.BlockSpec((1, 1, d), lambda i: (i // per_seq, 0, 0)),
            pl.BlockSpec((1, 1, d), lambda i: (i // per_seq, 0, 0)),
            pl.BlockSpec((N_HEADS, d), lambda i: (0, 0)),
            pl.BlockSpec((N_HEADS, 1), lambda i: (0, 0)),
            pl.BlockSpec((bm, bm), lambda i: (0, 0)),
        ],
        out_specs=[
            pl.BlockSpec((bm, d), lambda i: (i, 0)),
            pl.BlockSpec((1, N_HEADS, bm), lambda i: (i // per_seq, 0, i % per_seq)),
        ],
        out_shape=[
            jax.ShapeDtypeStruct((m, d), BF16),
            jax.ShapeDtypeStruct((bsz, N_HEADS, seq), F32),
        ],
        scratch_shapes=[pltpu.VMEM((N_HEADS, LANES), F32)],
        compiler_params=_cparams(("arbitrary",), 40),
    )(x2, g.reshape(1, d), scale, shift, wf_t, f_bias.reshape(N_HEADS, 1), tri)


def _mm_kernel(a_ref, w_ref, o_ref, *, relu2):
    y = jnp.dot(a_ref[...], w_ref[...], preferred_element_type=F32)
    if relu2:
        y = jnp.square(jnp.maximum(y, 0.0))
    o_ref[...] = y.astype(o_ref.dtype)


def _matmul(a, w, out_dtype, relu2=False):
    m, k = a.shape
    n = w.shape[1]
    bm = min(1024, m)
    bn = min(1024, n)
    return pl.pallas_call(
        functools.partial(_mm_kernel, relu2=relu2),
        grid=(m // bm, n // bn),
        in_specs=[
            pl.BlockSpec((bm, k), lambda i, j: (i, 0)),
            pl.BlockSpec((k, bn), lambda i, j: (0, j)),
        ],
        out_specs=pl.BlockSpec((bm, bn), lambda i, j: (i, j)),
        out_shape=jax.ShapeDtypeStruct((m, n), out_dtype),
        compiler_params=_cparams(("arbitrary", "arbitrary"), 52),
    )(a, w)


def _mm_acc_kernel(a_ref, w_ref, o_ref):
    y = jnp.dot(a_ref[...], w_ref[...], preferred_element_type=F32)

    @pl.when(pl.program_id(2) == 0)
    def _():
        o_ref[...] = y

    @pl.when(pl.program_id(2) != 0)
    def _():
        o_ref[...] += y


def _matmul_ksplit(a, w):
    m, k = a.shape
    n = w.shape[1]
    bm = min(1024, m)
    bn = min(1024, n)
    bk = min(2048, k)
    return pl.pallas_call(
        _mm_acc_kernel,
        grid=(m // bm, n // bn, k // bk),
        in_specs=[
            pl.BlockSpec((bm, bk), lambda i, j, q: (i, q)),
            pl.BlockSpec((bk, bn), lambda i, j, q: (q, j)),
        ],
        out_specs=pl.BlockSpec((bm, bn), lambda i, j, q: (i, j)),
        out_shape=jax.ShapeDtypeStruct((m, n), F32),
        compiler_params=_cparams(("arbitrary", "arbitrary", "arbitrary"), 52),
    )(a, w)


def _outproj_kernel(a0_ref, a1_ref, a2_ref, a3_ref, w_ref, o_ref):
    acc = None
    for idx, a_ref in enumerate((a0_ref, a1_ref, a2_ref, a3_ref)):
        part = jnp.dot(a_ref[...], w_ref[idx * D_GROUP:(idx + 1) * D_GROUP, :], preferred_element_type=F32)
        acc = part if acc is None else acc + part
    o_ref[...] = acc


def _outproj(parts, w):
    m = parts[0].shape[0]
    k, n = w.shape
    bm = min(1024, m)
    bn = min(1024, n)
    a_spec = pl.BlockSpec((bm, D_GROUP), lambda i, j: (i, 0))
    return pl.pallas_call(
        _outproj_kernel,
        grid=(m // bm, n // bn),
        in_specs=[a_spec, a_spec, a_spec, a_spec, pl.BlockSpec((k, bn), lambda i, j: (0, j))],
        out_specs=pl.BlockSpec((bm, bn), lambda i, j: (i, j)),
        out_shape=jax.ShapeDtypeStruct((m, n), F32),
        compiler_params=_cparams(("arbitrary", "arbitrary"), 52),
    )(*parts, w)


def _postnorm_kernel(x_ref, y_ref, g_ref, gt_ref, o_ref):
    y = y_ref[...]
    r = lax.rsqrt(jnp.mean(y * y, axis=-1, keepdims=True) + EPS)
    o_ref[...] = x_ref[...] + gt_ref[0] * (y * r * g_ref[...])


def _postnorm_residual(x2, y, g, gate, seq):
    m, d = x2.shape
    bm = min(256, seq)
    per_seq = seq // bm
    row = pl.BlockSpec((bm, d), lambda i: (i, 0))
    return pl.pallas_call(
        _postnorm_kernel,
        grid=(m // bm,),
        in_specs=[row, row, pl.BlockSpec((1, d), lambda i: (0, 0)),
                  pl.BlockSpec((1, 1, d), lambda i: (i // per_seq, 0, 0))],
        out_specs=row,
        out_shape=jax.ShapeDtypeStruct((m, d), F32),
        compiler_params=_cparams(("arbitrary",), 40),
    )(x2, y, g.reshape(1, d), gate)


def _layer_norm_rows(h, g, b):
    mu = jnp.mean(h, axis=-1, keepdims=True)
    hc = h - mu
    var = jnp.mean(hc * hc, axis=-1, keepdims=True)
    return hc * lax.rsqrt(var + EPS) * g + b


def _conv_kernel(a_ref, b_ref, ah_ref, bh_ref, w_ref, cb_ref, lg_ref, lb_ref, o_ref, buf_ref, acc_ref, *, bs):
    i = pl.program_id(1)
    glu = a_ref[0].astype(F32) * jax.nn.sigmoid(b_ref[0].astype(F32))
    halo = ah_ref[0].astype(F32) * jax.nn.sigmoid(bh_ref[0].astype(F32))
    halo = jnp.where(i == 0, 0.0, halo)
    n_cb = D_GROUP // LANES
    for cb in range(n_cb):
        cs = slice(cb * LANES, (cb + 1) * LANES)
        buf_ref[cb, 0:CONV_HALO, :] = halo[:, cs]
        buf_ref[cb, CONV_HALO:CONV_HALO + bs, :] = glu[:, cs]
    first = CONV_HALO - (CONV_WIDTH - 1)
    for cb in range(n_cb):
        cs = slice(cb * LANES, (cb + 1) * LANES)
        acc = jnp.broadcast_to(cb_ref[:, cs], (bs, LANES))
        for j in range(CONV_WIDTH):
            acc = acc + w_ref[j:j + 1, cs] * buf_ref[cb, first + j:first + j + bs, :]
        acc_ref[:, cs] = acc
    h = _layer_norm_rows(acc_ref[...], lg_ref[...], lb_ref[...])
    o_ref[0] = (h * jax.nn.sigmoid(h)).astype(BF16)


def _conv_module(z3, conv_w, conv_b, ln_g, ln_b):
    bsz, seq, _ = z3.shape
    bs = min(128, seq)
    hb = bs // CONV_HALO
    vec = lambda v: v.reshape(1, D_GROUP)
    tile = lambda col: pl.BlockSpec((1, bs, D_GROUP), lambda b, i: (b, i, col))
    halo = lambda col: pl.BlockSpec((1, CONV_HALO, D_GROUP), lambda b, i: (b, jnp.maximum(i * hb - 1, 0), col))
    full = lambda r: pl.BlockSpec((r, D_GROUP), lambda b, i: (0, 0))
    return pl.pallas_call(
        functools.partial(_conv_kernel, bs=bs),
        grid=(bsz, seq // bs),
        in_specs=[tile(Z_GLU_A // D_GROUP), tile(Z_GLU_B // D_GROUP),
                  halo(Z_GLU_A // D_GROUP), halo(Z_GLU_B // D_GROUP),
                  full(CONV_WIDTH), full(1), full(1), full(1)],
        out_specs=pl.BlockSpec((1, bs, D_GROUP), lambda b, i: (b, i, 0)),
        out_shape=jax.ShapeDtypeStruct((bsz, seq, D_GROUP), BF16),
        scratch_shapes=[pltpu.VMEM((D_GROUP // LANES, CONV_HALO + bs, LANES), F32),
                        pltpu.VMEM((bs, D_GROUP), F32)],
        compiler_params=_cparams(("arbitrary", "arbitrary"), 32),
    )(z3, z3, z3, z3, conv_w, vec(conv_b), vec(ln_g), vec(ln_b))


def _fox_kernel(q_ref, k_ref, v_ref, cs_ref, o_ref, *, blk):
    i = pl.program_id(2)
    q = q_ref[0]

    def step(j, carry, masked):
        m, l, acc = carry
        start = pl.multiple_of(j * blk, blk)
        kj = k_ref[0, pl.ds(start, blk), :]
        vj = v_ref[0, pl.ds(start, blk), :]
        s = lax.dot_general(q, kj, (((1,), (1,)), ((), ())), preferred_element_type=F32)
        s = s - cs_ref[0, j]
        if masked:
            row = lax.broadcasted_iota(jnp.int32, (blk, blk), 0)
            col = lax.broadcasted_iota(jnp.int32, (blk, blk), 1)
            s = jnp.where(col <= row, s, -jnp.inf)
        m_new = jnp.maximum(m, jnp.max(s, axis=-1, keepdims=True))
        alpha = jnp.exp(m - m_new)
        p = jnp.exp(s - m_new)
        l = alpha * l + jnp.sum(p, axis=-1, keepdims=True)
        acc = alpha * acc + jnp.dot(p.astype(BF16), vj, preferred_element_type=F32)
        return m_new, l, acc

    init = (jnp.full((blk, 1), -jnp.inf, F32), jnp.zeros((blk, 1), F32), jnp.zeros((blk, HEAD_DIM), F32))
    carry = lax.fori_loop(0, i, lambda j, c: step(j, c, False), init)
    _, l, acc = step(i, carry, True)
    o_ref[0] = (acc * (1.0 / l)).astype(BF16)


def _forgetting_attention(z3, cum):
    bsz, seq, _ = z3.shape
    blk = min(512, seq)
    nb = seq // blk
    cs = cum.reshape(bsz * N_HEADS, nb, 1, blk)
    col = lambda off: off // HEAD_DIM
    kv = lambda off: pl.BlockSpec((1, seq, HEAD_DIM), lambda b, h, i: (b, 0, col(off) + h))
    return pl.pallas_call(
        functools.partial(_fox_kernel, blk=blk),
        grid=(bsz, N_HEADS, nb),
        in_specs=[pl.BlockSpec((1, blk, HEAD_DIM), lambda b, h, i: (b, i, col(Z_FQ) + h)),
                  kv(Z_FK), kv(Z_FV),
                  pl.BlockSpec((1, nb, 1, blk), lambda b, h, i: (b * N_HEADS + h, 0, 0, 0))],
        out_specs=pl.BlockSpec((1, blk, HEAD_DIM), lambda b, h, i: (b, i, h)),
        out_shape=jax.ShapeDtypeStruct((bsz, seq, D_GROUP), BF16),
        compiler_params=_cparams(("arbitrary", "arbitrary", "arbitrary"), 32),
    )(z3, z3, z3, cs)


def _ret_kernel(q_ref, k_ref, v_ref, g_ref, cos_ref, sin_ref, dec_ref, qw_ref, kw_ref, cd_ref, bd_ref,
                o_ref, state_ref, *, chunk):
    n = pl.program_id(1)

    @pl.when(n == 0)
    def _():
        state_ref[...] = jnp.zeros_like(state_ref)

    cos = cos_ref[...]
    sin = sin_ref[...]
    lane = lax.broadcasted_iota(jnp.int32, (chunk, LANES), 1)
    first_half = (lane & (RET_K_DIM // 2)) == 0
    low_head = lane < RET_K_DIM

    def rotate(x):
        swapped = jnp.where(first_half, pltpu.roll(x, LANES - RET_K_DIM // 2, 1), pltpu.roll(x, RET_K_DIM // 2, 1))
        return x * cos + swapped * sin

    for pr in range(N_HEADS // 2):
        ls = slice(pr * LANES, (pr + 1) * LANES)
        qr = rotate(q_ref[0, :, ls].astype(F32))
        kr = rotate(k_ref[0, :, ls].astype(F32))
        kb = kr.astype(BF16)
        v_pair = v_ref[0, :, pr * 2 * HEAD_DIM:(pr + 1) * 2 * HEAD_DIM]
        state = state_ref[pr]
        cross = jnp.dot((qr * qw_ref[:, ls]).astype(BF16), state.astype(BF16), preferred_element_type=F32)
        for a in range(2):
            head = 2 * pr + a
            hs = slice(head * HEAD_DIM, (head + 1) * HEAD_DIM)
            qa = jnp.where(low_head if a == 0 else jnp.logical_not(low_head), qr, 0.0).astype(BF16)
            scores = lax.dot_general(qa, kb, (((1,), (1,)), ((), ())), preferred_element_type=F32) * dec_ref[head]
            inner = jnp.dot(scores.astype(BF16), v_pair[:, a * HEAD_DIM:(a + 1) * HEAD_DIM],
                            preferred_element_type=F32)
            y = inner + cross[:, a * HEAD_DIM:(a + 1) * HEAD_DIM]
            mu = jnp.mean(y, axis=-1, keepdims=True)
            yc = y - mu
            var = jnp.mean(yc * yc, axis=-1, keepdims=True)
            yn = yc * lax.rsqrt(var + GN_EPS)
            gate = g_ref[0, :, hs].astype(F32)
            o_ref[0, :, hs] = (gate * jax.nn.sigmoid(gate) * yn).astype(BF16)
        kw_t = (kr * kw_ref[:, ls]).T.astype(BF16)
        kv = jnp.dot(kw_t, v_pair, preferred_element_type=F32)
        state_ref[pr] = state * cd_ref[pr] + kv * bd_ref[...]


def _retention_tables(seq, chunk):
    half = RET_K_DIM // 2
    inv = 1.0 / (ROPE_BASE ** jnp.linspace(0.0, 1.0, half, dtype=F32))
    ang = jnp.arange(seq).astype(F32)[:, None] * inv[None, :]
    cos = jnp.tile(jnp.cos(ang), (1, LANES // half))
    sin = jnp.tile(jnp.concatenate([-jnp.sin(ang), jnp.sin(ang)], axis=-1), (1, LANES // RET_K_DIM))
    log_gamma = jnp.log(1.0 - 2.0 ** (-5.0 - jnp.arange(N_HEADS, dtype=F32)))
    idx = jnp.arange(chunk, dtype=F32)
    rel = idx[:, None] - idx[None, :]
    decay = jnp.where(rel >= 0, jnp.exp(log_gamma[:, None, None] * jnp.maximum(rel, 0.0)), 0.0)
    k_w = jnp.exp(log_gamma[:, None] * (chunk - 1.0 - idx)[None, :])
    q_w = jnp.exp(log_gamma[:, None] * (idx + 1.0)[None, :])
    per_lane = lambda t: jnp.repeat(t.T, RET_K_DIM, axis=1)
    chunk_decay = jnp.exp(log_gamma * chunk)
    cd = jnp.broadcast_to(jnp.repeat(chunk_decay, RET_K_DIM).reshape(N_HEADS // 2, LANES, 1),
                          (N_HEADS // 2, LANES, 2 * HEAD_DIM))
    r = jnp.arange(LANES)[:, None] // RET_K_DIM
    c = jnp.arange(2 * HEAD_DIM)[None, :] // HEAD_DIM
    bd = (r == c).astype(F32)
    return cos, sin, decay, per_lane(q_w), per_lane(k_w), cd, bd


def _retention(z3):
    bsz, seq, _ = z3.shape
    chunk = min(256, seq)
    cos, sin, decay, q_w, k_w, cd, bd = _retention_tables(seq, chunk)
    d_qk = N_HEADS * RET_K_DIM
    const = lambda shape: pl.BlockSpec(shape, lambda b, n: (0,) * len(shape))
    return pl.pallas_call(
        functools.partial(_ret_kernel, chunk=chunk),
        grid=(bsz, seq // chunk),
        in_specs=[pl.BlockSpec((1, chunk, d_qk), lambda b, n: (b, n, Z_RQ // d_qk)),
                  pl.BlockSpec((1, chunk, d_qk), lambda b, n: (b, n, Z_RK // d_qk)),
                  pl.BlockSpec((1, chunk, D_GROUP), lambda b, n: (b, n, Z_RV // D_GROUP)),
                  pl.BlockSpec((1, chunk, D_GROUP), lambda b, n: (b, n, Z_RG // D_GROUP)),
                  pl.BlockSpec((chunk, LANES), lambda b, n: (n, 0)),
                  pl.BlockSpec((chunk, LANES), lambda b, n: (n, 0)),
                  const((N_HEADS, chunk, chunk)), const((chunk, d_qk)), const((chunk, d_qk)),
                  const((N_HEADS // 2, LANES, 2 * HEAD_DIM)), const((LANES, 2 * HEAD_DIM))],
        out_specs=pl.BlockSpec((1, chunk, D_GROUP), lambda b, n: (b, n, 0)),
        out_shape=jax.ShapeDtypeStruct((bsz, seq, D_GROUP), BF16),
        scratch_shapes=[pltpu.VMEM((N_HEADS // 2, LANES, 2 * HEAD_DIM), F32)],
        compiler_params=_cparams(("arbitrary", "arbitrary"), 32),
    )(z3, z3, z3, z3, cos, sin, decay, q_w, k_w, cd, bd)


def _gelu_tanh(x):
    return 0.5 * x * (1.0 + jnp.tanh(math.sqrt(2.0 / math.pi) * (x + 0.044715 * (x * x * x))))


def _sg_kernel(u_ref, v_ref, w_ref, bt_ref, lg_ref, lb_ref, o_ref, *, bs):
    u = _gelu_tanh(u_ref[0].astype(F32))
    v = _layer_norm_rows(_gelu_tanh(v_ref[0].astype(F32)), lg_ref[...], lb_ref[...]).astype(BF16)
    row = lax.broadcasted_iota(jnp.int32, (SG_CHUNK, SG_CHUNK), 0)
    col = lax.broadcasted_iota(jnp.int32, (SG_CHUNK, SG_CHUNK), 1)
    causal = col <= row
    for g in range(N_HEADS):
        cs = slice(g * HEAD_DIM, (g + 1) * HEAD_DIM)
        w_g = jnp.where(causal, w_ref[g], 0.0).astype(BF16)
        bias = bt_ref[:, g:g + 1]
        for r in range(bs // SG_CHUNK):
            rs = slice(r * SG_CHUNK, (r + 1) * SG_CHUNK)
            mixed = jnp.dot(w_g, v[rs, cs], preferred_element_type=F32) + bias
            o_ref[0, rs, cs] = (u[rs, cs] * mixed).astype(BF16)


def _spatial_gating(z3, sg_w, sg_b, ln_g, ln_b):
    bsz, seq, _ = z3.shape
    bs = min(512, seq)
    vec = lambda t: t.reshape(1, D_GROUP)
    tile = lambda off: pl.BlockSpec((1, bs, D_GROUP), lambda b, i: (b, i, off // D_GROUP))
    return pl.pallas_call(
        functools.partial(_sg_kernel, bs=bs),
        grid=(bsz, seq // bs),
        in_specs=[tile(Z_SGU), tile(Z_SGV),
                  pl.BlockSpec((N_HEADS, SG_CHUNK, SG_CHUNK), lambda b, i: (0, 0, 0)),
                  pl.BlockSpec((SG_CHUNK, N_HEADS), lambda b, i: (0, 0)),
                  pl.BlockSpec((1, D_GROUP), lambda b, i: (0, 0)),
                  pl.BlockSpec((1, D_GROUP), lambda b, i: (0, 0))],
        out_specs=pl.BlockSpec((1, bs, D_GROUP), lambda b, i: (b, i, 0)),
        out_shape=jax.ShapeDtypeStruct((bsz, seq, D_GROUP), BF16),
        compiler_params=_cparams(("arbitrary", "arbitrary"), 32),
    )(z3, z3, sg_w, sg_b.T, vec(ln_g), vec(ln_b))


def _layer(x2, mod, bsz, seq, mix_pre_g, mix_post_g, w_in, fox_f_bias, conv_w, conv_b, conv_ln_g, conv_ln_b,
           sg_w, sg_b, sg_ln_g, sg_ln_b, w_out, ffn_pre_g, ffn_post_g, w_ff1, w_ff2):
    d = x2.shape[1]
    sh1, sc1, gt1, sh2, sc2, gt2 = [t.reshape(bsz, 1, d) for t in jnp.split(mod, 6, axis=-1)]
    scale_cols = jnp.ones((w_in.shape[1],), F32)
    scale_cols = scale_cols.at[Z_FQ:Z_FQ + D_GROUP].set(HEAD_DIM ** -0.5)
    rk = FF_OFF + N_HEADS + N_HEADS * RET_K_DIM
    scale_cols = scale_cols.at[rk:rk + N_HEADS * RET_K_DIM].set(RET_K_DIM ** -0.5)
    w_scaled = w_in * scale_cols[None, :]
    w_main = jnp.concatenate([w_scaled[:, :FF_OFF], w_scaled[:, FF_OFF + N_HEADS:]], axis=1).astype(BF16)
    wf_t = w_in[:, FF_OFF:FF_OFF + N_HEADS].T.astype(BF16)

    h, cum = _prenorm_forget(x2, mix_pre_g, sc1, sh1, wf_t, fox_f_bias, seq)
    z3 = _matmul(h, w_main, BF16).reshape(bsz, seq, D_Z)
    y_conv = _conv_module(z3, conv_w, conv_b, conv_ln_g, conv_ln_b)
    y_fox = _forgetting_attention(z3, cum)
    y_ret = _retention(z3)
    y_sg = _spatial_gating(z3, sg_w, sg_b, sg_ln_g, sg_ln_b)
    parts = [t.reshape(bsz * seq, D_GROUP) for t in (y_conv, y_fox, y_ret, y_sg)]
    y = _outproj(parts, w_out.astype(BF16))
    x2 = _postnorm_residual(x2, y, mix_post_g, gt1, seq)

    h = _prenorm(x2, ffn_pre_g, sc2, sh2, seq)
    u = _matmul(h, w_ff1.astype(BF16), BF16, relu2=True)
    y = _matmul_ksplit(u, w_ff2.astype(BF16))
    return _postnorm_residual(x2, y, ffn_post_g, gt2, seq)


def kernel(x, c, ada_w, ada_b, mix_pre_g, mix_post_g, w_in, fox_f_bias, conv_w, conv_b, conv_ln_g, conv_ln_b,
           sg_w, sg_b, sg_ln_g, sg_ln_b, w_out, ffn_pre_g, ffn_post_g, w_ff1, w_ff2):
    bsz, seq, d = x.shape
    mod = _modulation(c, ada_w, ada_b)
    x2 = x.reshape(bsz * seq, d)
    for l in range(ada_w.shape[0]):
        x2 = _layer(x2, mod[l], bsz, seq, mix_pre_g[l], mix_post_g[l], w_in[l], fox_f_bias[l], conv_w[l],
                    conv_b[l], conv_ln_g[l], conv_ln_b[l], sg_w[l], sg_b[l], sg_ln_g[l], sg_ln_b[l], w_out[l],
                    ffn_pre_g[l], ffn_post_g[l], w_ff1[l], w_ff2[l])
    return x2.reshape(bsz, seq, d)
```

```python
import functools
import math

import jax
import jax.numpy as jnp
from jax import lax
from jax.experimental import pallas as pl
from jax.experimental.pallas import tpu as pltpu

F32 = jnp.float32
BF16 = jnp.bfloat16

D_GROUP = 1024
N_HEADS = 8
HEAD_DIM = D_GROUP // N_HEADS
CONV_WIDTH = 31
CONV_HALO = 32
RET_K_DIM = HEAD_DIM // 2
SG_CHUNK = 128
ROPE_BASE = 10000.0
EPS = 1e-6
GN_EPS = 1e-5
LOG2E = math.log2(math.e)

D_ZHALF = 5120
FF_OFF = 5120
ZLO_GLU_A, ZLO_GLU_B, ZLO_FQ, ZLO_FK, ZLO_FV = 0, 1024, 2048, 3072, 4096
ZHI_RQ, ZHI_RK, ZHI_RV, ZHI_RG, ZHI_SGU, ZHI_SGV = 0, 512, 1024, 2048, 3072, 4096

MIB = 1024 * 1024
LANES = 128


def _cparams(semantics, vmem_mib):
    return pltpu.CompilerParams(dimension_semantics=semantics, vmem_limit_bytes=vmem_mib * MIB)


def _split3(x):
    p1 = x.astype(BF16)
    r1 = x - p1.astype(F32)
    p2 = r1.astype(BF16)
    p3 = (r1 - p2.astype(F32)).astype(BF16)
    return p1, p2, p3


def _mod_kernel(ct_ref, w_ref, b_ref, o_ref, sb_ref, *, bsz):
    @pl.when((pl.program_id(0) == 0) & (pl.program_id(1) == 0))
    def _():
        for b in range(bsz):
            cb = ct_ref[:, b:b + 1]
            sb_ref[b] = jnp.broadcast_to(cb * jax.nn.sigmoid(cb), sb_ref.shape[1:])

    o_ref[0] = jnp.zeros(o_ref.shape[1:], F32)
    tn = w_ref.shape[2]
    for cg in range(tn // LANES):
        cs = slice(cg * LANES, (cg + 1) * LANES)
        w = w_ref[0, :, cs]
        for b in range(bsz):
            o_ref[0, b:b + 1, cs] = jnp.sum(w * sb_ref[b], axis=0, keepdims=True) + b_ref[0, :, cs]


def _modulation(c, ada_w, ada_b):
    n_layers, d, n = ada_w.shape
    bsz = c.shape[0]
    rows = 8
    tn = 512
    out = pl.pallas_call(
        functools.partial(_mod_kernel, bsz=bsz),
        grid=(n_layers, n // tn),
        in_specs=[
            pl.BlockSpec((d, bsz), lambda l, j: (0, 0)),
            pl.BlockSpec((1, d, tn), lambda l, j: (l, 0, j)),
            pl.BlockSpec((1, 1, tn), lambda l, j: (l, 0, j)),
        ],
        out_specs=pl.BlockSpec((1, rows, tn), lambda l, j: (l, 0, j)),
        out_shape=jax.ShapeDtypeStruct((n_layers, rows, n), F32),
        scratch_shapes=[pltpu.VMEM((bsz, d, LANES), F32)],
        compiler_params=_cparams(("arbitrary", "arbitrary"), 40),
        name="adaln_mod",
    )(c.T, ada_w, ada_b.reshape(n_layers, 1, n))
    return out[:, :bsz]


def _log_sigmoid(x):
    return jnp.minimum(x, 0.0) - jnp.log(1.0 + jnp.exp(-jnp.abs(x)))


def _rows_kernel(*refs, has_post, has_pre, has_forget, tiles_per_seq):
    it = iter(refs)
    x_ref = next(it)
    if has_post:
        y_ref, gpost_ref, gate_ref = next(it), next(it), next(it)
    if has_pre:
        gpre_ref, sc_ref, sh_ref = next(it), next(it), next(it)
    if has_forget:
        wf_ref, fb_ref, tri_ref = next(it), next(it), next(it)
    if has_post:
        xo_ref = next(it)
    if has_pre:
        h_ref = next(it)
    if has_forget:
        c3_ref, carry_ref = next(it), next(it)

    x = x_ref[...]
    if has_post:
        y = y_ref[...]
        r = lax.rsqrt(jnp.mean(y * y, axis=-1, keepdims=True) + EPS)
        x = x + gate_ref[0] * (y * r * gpost_ref[...])
        xo_ref[...] = x
    if not has_pre:
        return
    r = lax.rsqrt(jnp.mean(x * x, axis=-1, keepdims=True) + EPS)
    hb = (x * r * (gpre_ref[...] * (1.0 + sc_ref[0])) + sh_ref[0]).astype(BF16)
    h_ref[...] = hb
    if not has_forget:
        return
    logf = _log_sigmoid(jnp.dot(hb, wf_ref[...], preferred_element_type=F32) + fb_ref[...])
    tri = tri_ref[...]
    local = sum(jnp.dot(tri, p, preferred_element_type=F32) for p in _split3(logf))

    @pl.when(pl.program_id(0) % tiles_per_seq == 0)
    def _():
        carry_ref[...] = jnp.zeros_like(carry_ref)

    cum = carry_ref[...] + local
    bm = cum.shape[0]
    carry_ref[...] = cum[bm - 1:bm, :]
    n1, n2, n3 = _split3(cum * (-LOG2E))
    lane = lax.broadcasted_iota(jnp.int32, cum.shape, 1)
    zero = jnp.zeros_like(n1)
    c3_ref[0] = jnp.where(lane < N_HEADS, n1,
                          jnp.where(lane < 2 * N_HEADS, n2, jnp.where(lane < 3 * N_HEADS, n3, zero)))


def _rows(x2, seq, post=None, pre=None, forget=None):
    m, d = x2.shape
    bsz = m // seq
    bm = min(256, seq)
    per_seq = seq // bm
    row = pl.BlockSpec((bm, d), lambda i: (i, 0))
    vec = pl.BlockSpec((1, d), lambda i: (0, 0))
    per_batch = pl.BlockSpec((1, 1, d), lambda i: (i // per_seq, 0, 0))
    args, in_specs, out_specs, out_shape, scratch = [x2], [row], [], [], []
    if post is not None:
        y, g_post, gate = post
        args += [y, g_post.reshape(1, d), gate]
        in_specs += [row, vec, per_batch]
    if pre is not None:
        g_pre, scale, shift = pre
        args += [g_pre.reshape(1, d), scale, shift]
        in_specs += [vec, per_batch, per_batch]
    if forget is not None:
        wf, fb = forget
        tri = (lax.broadcasted_iota(jnp.int32, (bm, bm), 1) <= lax.broadcasted_iota(jnp.int32, (bm, bm), 0)).astype(BF16)
        args += [wf, fb, tri]
        in_specs += [pl.BlockSpec((d, LANES), lambda i: (0, 0)), pl.BlockSpec((1, LANES), lambda i: (0, 0)),
                     pl.BlockSpec((bm, bm), lambda i: (0, 0))]
    if post is not None:
        out_specs.append(row)
        out_shape.append(jax.ShapeDtypeStruct((m, d), F32))
    if pre is not None:
        out_specs.append(row)
        out_shape.append(jax.ShapeDtypeStruct((m, d), BF16))
    if forget is not None:
        out_specs.append(pl.BlockSpec((1, bm, LANES), lambda i: (i // per_seq, i % per_seq, 0)))
        out_shape.append(jax.ShapeDtypeStruct((bsz, seq, LANES), BF16))
        scratch.append(pltpu.VMEM((1, LANES), F32))
    return pl.pallas_call(
        functools.partial(_rows_kernel, has_post=post is not None, has_pre=pre is not None,
                          has_forget=forget is not None, tiles_per_seq=per_seq),
        grid=(m // bm,),
        in_specs=in_specs, out_specs=out_specs, out_shape=out_shape, scratch_shapes=scratch,
        compiler_params=_cparams(("arbitrary",), 48),
        name="rows_norm",
    )(*args)


def _mm_ws_kernel(*refs, n_a, has_scale, relu2):
    a_refs = refs[:n_a]
    w_ref = refs[n_a]
    s_ref = refs[n_a + 1] if has_scale else None
    o_ref, wb_ref = refs[-2], refs[-1]

    @pl.when(pl.program_id(1) == 0)
    def _():
        w = w_ref[0]
        if has_scale:
            w = w * s_ref[...]
        wb_ref[...] = w.astype(BF16)

    ksz = wb_ref.shape[0] // n_a
    acc = None
    for t, a_ref in enumerate(a_refs):
        part = jnp.dot(a_ref[...], wb_ref[t * ksz:(t + 1) * ksz, :], preferred_element_type=F32)
        acc = part if acc is None else acc + part
    if relu2:
        acc = jnp.square(jnp.maximum(acc, 0.0))
    o_ref[...] = acc.astype(o_ref.dtype)


def _matmul_ws(a_parts, w_all, layer, n_out, out_dtype, col_scale=None, relu2=False):
    m = a_parts[0].shape[0]
    k = w_all.shape[1]
    ka = k // len(a_parts)
    bm = min(1024, m)
    bn = min(512, n_out)
    in_specs = [pl.BlockSpec((bm, ka), lambda j, i: (i, 0)) for _ in a_parts]
    in_specs.append(pl.BlockSpec((1, k, bn), lambda j, i: (layer, 0, j)))
    args = list(a_parts) + [w_all]
    if col_scale is not None:
        in_specs.append(pl.BlockSpec((1, bn), lambda j, i: (0, j)))
        args.append(col_scale.reshape(1, n_out))
    return pl.pallas_call(
        functools.partial(_mm_ws_kernel, n_a=len(a_parts), has_scale=col_scale is not None, relu2=relu2),
        grid=(n_out // bn, m // bm),
        in_specs=in_specs,
        out_specs=pl.BlockSpec((bm, bn), lambda j, i: (i, j)),
        out_shape=jax.ShapeDtypeStruct((m, n_out), out_dtype),
        scratch_shapes=[pltpu.VMEM((k, bn), BF16)],
        compiler_params=_cparams(("arbitrary", "arbitrary"), 48),
        name="matmul_ws",
    )(*args)


def _mm_acc_kernel(a_ref, w_ref, o_ref):
    y = jnp.dot(a_ref[...], w_ref[0], preferred_element_type=F32)

    @pl.when(pl.program_id(2) == 0)
    def _():
        o_ref[...] = y

    @pl.when(pl.program_id(2) != 0)
    def _():
        o_ref[...] += y


def _matmul_ksplit(a, w_all, layer):
    m, k = a.shape
    n = w_all.shape[2]
    bm = min(1024, m)
    bn = min(1024, n)
    bk = min(4096, k)
    return pl.pallas_call(
        _mm_acc_kernel,
        grid=(m // bm, n // bn, k // bk),
        in_specs=[
            pl.BlockSpec((bm, bk), lambda i, j, q: (i, q)),
            pl.BlockSpec((1, bk, bn), lambda i, j, q: (layer, q, j)),
        ],
        out_specs=pl.BlockSpec((bm, bn), lambda i, j, q: (i, j)),
        out_shape=jax.ShapeDtypeStruct((m, n), F32),
        compiler_params=_cparams(("arbitrary", "arbitrary", "arbitrary"), 52),
        name="matmul_ksplit",
    )(a, w_all)


def _layer_norm_rows(h, g, b):
    mu = jnp.mean(h, axis=-1, keepdims=True)
    hc = h - mu
    var = jnp.mean(hc * hc, axis=-1, keepdims=True)
    return hc * lax.rsqrt(var + EPS) * g + b


def _conv_kernel(a_ref, b_ref, ah_ref, bh_ref, w_ref, cb_ref, lg_ref, lb_ref, o_ref, buf_ref, acc_ref, *, bs):
    i = pl.program_id(1)
    glu = a_ref[0].astype(F32) * jax.nn.sigmoid(b_ref[0].astype(F32))
    halo = ah_ref[0].astype(F32) * jax.nn.sigmoid(bh_ref[0].astype(F32))
    halo = jnp.where(i == 0, 0.0, halo)
    n_cb = D_GROUP // LANES
    for cb in range(n_cb):
        cs = slice(cb * LANES, (cb + 1) * LANES)
        buf_ref[cb, 0:CONV_HALO, :] = halo[:, cs]
        buf_ref[cb, CONV_HALO:CONV_HALO + bs, :] = glu[:, cs]
    first = CONV_HALO - (CONV_WIDTH - 1)
    for cb in range(n_cb):
        cs = slice(cb * LANES, (cb + 1) * LANES)
        acc = jnp.broadcast_to(cb_ref[:, cs], (bs, LANES))
        for j in range(CONV_WIDTH):
            acc = acc + w_ref[j:j + 1, cs] * buf_ref[cb, first + j:first + j + bs, :]
        acc_ref[:, cs] = acc
    h = _layer_norm_rows(acc_ref[...], lg_ref[...], lb_ref[...])
    o_ref[0] = (h * jax.nn.sigmoid(h)).astype(BF16)


def _conv_module(z_lo, conv_w, conv_b, ln_g, ln_b):
    bsz, seq, _ = z_lo.shape
    bs = min(128, seq)
    hb = bs // CONV_HALO
    vec = lambda v: v.reshape(1, D_GROUP)
    tile = lambda col: pl.BlockSpec((1, bs, D_GROUP), lambda b, i: (b, i, col))
    halo = lambda col: pl.BlockSpec((1, CONV_HALO, D_GROUP), lambda b, i: (b, jnp.maximum(i * hb - 1, 0), col))
    full = lambda r: pl.BlockSpec((r, D_GROUP), lambda b, i: (0, 0))
    return pl.pallas_call(
        functools.partial(_conv_kernel, bs=bs),
        grid=(bsz, seq // bs),
        in_specs=[tile(ZLO_GLU_A // D_GROUP), tile(ZLO_GLU_B // D_GROUP),
                  halo(ZLO_GLU_A // D_GROUP), halo(ZLO_GLU_B // D_GROUP),
                  full(CONV_WIDTH), full(1), full(1), full(1)],
        out_specs=pl.BlockSpec((1, bs, D_GROUP), lambda b, i: (b, i, 0)),
        out_shape=jax.ShapeDtypeStruct((bsz, seq, D_GROUP), BF16),
        scratch_shapes=[pltpu.VMEM((D_GROUP // LANES, CONV_HALO + bs, LANES), F32),
                        pltpu.VMEM((bs, D_GROUP), F32)],
        compiler_params=_cparams(("arbitrary", "arbitrary"), 32),
        name="conv_module",
    )(z_lo, z_lo, z_lo, z_lo, conv_w, vec(conv_b), vec(ln_g), vec(ln_b))


def _fox_kernel(q_ref, k_ref, v_ref, c3_ref, o_ref, ka_ref, va_ref, s0_ref, s1_ref, p_ref, m_ref, acc_ref,
                *, blk, sub, rg):
    h = pl.program_id(1)
    i = pl.program_id(2)

    @pl.when(i == 0)
    def _():
        ka_ref[:, :HEAD_DIM] = k_ref[0]
        ka_ref[:, HEAD_DIM:] = c3_ref[0]
        va_ref[:, :HEAD_DIM] = v_ref[0]
        lane = lax.broadcasted_iota(jnp.int32, (va_ref.shape[0], HEAD_DIM), 1)
        va_ref[:, HEAD_DIM:] = (lane == 0).astype(BF16)

    lane = lax.broadcasted_iota(jnp.int32, (blk, HEAD_DIM), 1)
    pick = (lane == h) | (lane == h + N_HEADS) | (lane == h + 2 * N_HEADS)
    qa = jnp.concatenate([q_ref[0], pick.astype(BF16)], axis=1)
    m_ref[...] = jnp.full(m_ref.shape, -jnp.inf, F32)
    acc_ref[...] = jnp.zeros(acc_ref.shape, F32)

    def logits_to(s_ref, jsub, r0=0):
        start = pl.multiple_of(jsub * sub, sub)
        s_ref[r0:, :] = lax.dot_general(qa[r0:], ka_ref[pl.ds(start, sub), :], (((1,), (1,)), ((), ())),
                                        preferred_element_type=F32)

    def update(s_ref, jsub, mask_off=None, r0=0):
        for r in range(r0 // rg, blk // rg):
            rows = slice(r * rg, (r + 1) * rg)
            s = s_ref[rows, :]
            if mask_off is not None and r * rg < mask_off + sub - 1:
                row = lax.broadcasted_iota(jnp.int32, (rg, sub), 0) + r * rg
                col = lax.broadcasted_iota(jnp.int32, (rg, sub), 1) + mask_off
                s = jnp.where(col <= row, s, -jnp.inf)
            m_old = m_ref[rows, :]
            m_new = jnp.maximum(m_old, jnp.max(s, axis=-1, keepdims=True))
            m_ref[rows, :] = m_new
            alpha = jnp.exp2(m_old - m_new)
            for cg in range(sub // LANES):
                cs = slice(cg * LANES, (cg + 1) * LANES)
                p_ref[rows, cs] = jnp.exp2(s[:, cs] - m_new).astype(BF16)
            for cg in range(acc_ref.shape[1] // LANES):
                cs = slice(cg * LANES, (cg + 1) * LANES)
                acc_ref[rows, cs] = alpha * acc_ref[rows, cs]
        start = pl.multiple_of(jsub * sub, sub)
        acc_ref[r0:, :] += jnp.dot(p_ref[r0:, :], va_ref[pl.ds(start, sub), :], preferred_element_type=F32)

    per = blk // sub
    logits_to(s0_ref, 0)

    def body(t, carry):
        logits_to(s1_ref, per * t + 1)
        update(s0_ref, per * t)
        logits_to(s0_ref, per * t + 2)
        update(s1_ref, per * t + 1)
        return carry

    lax.fori_loop(0, i, body, 0)
    logits_to(s1_ref, per * i + 1, r0=sub)
    update(s0_ref, per * i, mask_off=0)
    update(s1_ref, per * i + 1, mask_off=sub, r0=sub)
    acc = acc_ref[...]
    o_ref[0] = (acc[:, :HEAD_DIM] * (1.0 / acc[:, HEAD_DIM:HEAD_DIM + 1])).astype(BF16)


def _forgetting_attention(z_lo, c3):
    bsz, seq, _ = z_lo.shape
    blk = min(1024, seq)
    sub = blk // 2
    rg = min(128, blk)
    col = lambda off: off // HEAD_DIM
    kv = lambda off: pl.BlockSpec((1, seq, HEAD_DIM), lambda b, h, i: (b, 0, col(off) + h))
    return pl.pallas_call(
        functools.partial(_fox_kernel, blk=blk, sub=sub, rg=rg),
        grid=(bsz, N_HEADS, seq // blk),
        in_specs=[pl.BlockSpec((1, blk, HEAD_DIM), lambda b, h, i: (b, i, col(ZLO_FQ) + h)),
                  kv(ZLO_FK), kv(ZLO_FV),
                  pl.BlockSpec((1, seq, LANES), lambda b, h, i: (b, 0, 0))],
        out_specs=pl.BlockSpec((1, blk, HEAD_DIM), lambda b, h, i: (b, i, h)),
        out_shape=jax.ShapeDtypeStruct((bsz, seq, D_GROUP), BF16),
        scratch_shapes=[pltpu.VMEM((seq, 2 * HEAD_DIM), BF16), pltpu.VMEM((seq, 2 * HEAD_DIM), BF16),
                        pltpu.VMEM((blk, sub), F32), pltpu.VMEM((blk, sub), F32), pltpu.VMEM((blk, sub), BF16),
                        pltpu.VMEM((blk, LANES), F32),
                        pltpu.VMEM((blk, 2 * HEAD_DIM), F32)],
        compiler_params=_cparams(("arbitrary", "arbitrary", "arbitrary"), 48),
        name="forgetting_attention",
    )(z_lo, z_lo, z_lo, c3)


def _ret_kernel(q_ref, k_ref, v_ref, g_ref, cos_ref, sin_ref, dec_ref, qw_ref, kw_ref, cd_ref, bd_ref,
                o_ref, state_ref, *, chunk):
    n = pl.program_id(1)

    @pl.when(n == 0)
    def _():
        state_ref[...] = jnp.zeros_like(state_ref)

    cos = cos_ref[...]
    sin = sin_ref[...]
    lane = lax.broadcasted_iota(jnp.int32, (chunk, LANES), 1)
    first_half = (lane & (RET_K_DIM // 2)) == 0
    low_head = lane < RET_K_DIM

    def rotate(x):
        swapped = jnp.where(first_half, pltpu.roll(x, LANES - RET_K_DIM // 2, 1), pltpu.roll(x, RET_K_DIM // 2, 1))
        return x * cos + swapped * sin

    for pr in range(N_HEADS // 2):
        ls = slice(pr * LANES, (pr + 1) * LANES)
        qr = rotate(q_ref[0, :, ls].astype(F32))
        kr = rotate(k_ref[0, :, ls].astype(F32))
        kb = kr.astype(BF16)
        v_pair = v_ref[0, :, pr * 2 * HEAD_DIM:(pr + 1) * 2 * HEAD_DIM]
        state = state_ref[pr]
        cross = jnp.dot((qr * qw_ref[:, ls]).astype(BF16), state.astype(BF16), preferred_element_type=F32)
        for a in range(2):
            head = 2 * pr + a
            hs = slice(head * HEAD_DIM, (head + 1) * HEAD_DIM)
            qa = jnp.where(low_head if a == 0 else jnp.logical_not(low_head), qr, 0.0).astype(BF16)
            scores = lax.dot_general(qa, kb, (((1,), (1,)), ((), ())), preferred_element_type=F32) * dec_ref[head]
            inner = jnp.dot(scores.astype(BF16), v_pair[:, a * HEAD_DIM:(a + 1) * HEAD_DIM],
                            preferred_element_type=F32)
            y = inner + cross[:, a * HEAD_DIM:(a + 1) * HEAD_DIM]
            mu = jnp.mean(y, axis=-1, keepdims=True)
            yc = y - mu
            var = jnp.mean(yc * yc, axis=-1, keepdims=True)
            yn = yc * lax.rsqrt(var + GN_EPS)
            gate = g_ref[0, :, hs].astype(F32)
            o_ref[0, :, hs] = (gate * jax.nn.sigmoid(gate) * yn).astype(BF16)
        kw_t = (kr * kw_ref[:, ls]).T.astype(BF16)
        kv = jnp.dot(kw_t, v_pair, preferred_element_type=F32)
        state_ref[pr] = state * cd_ref[pr] + kv * bd_ref[...]


def _retention_tables(seq, chunk):
    half = RET_K_DIM // 2
    inv = 1.0 / (ROPE_BASE ** jnp.linspace(0.0, 1.0, half, dtype=F32))
    ang = jnp.arange(seq).astype(F32)[:, None] * inv[None, :]
    cos = jnp.tile(jnp.cos(ang), (1, LANES // half))
    sin = jnp.tile(jnp.concatenate([-jnp.sin(ang), jnp.sin(ang)], axis=-1), (1, LANES // RET_K_DIM))
    log_gamma = jnp.log(1.0 - 2.0 ** (-5.0 - jnp.arange(N_HEADS, dtype=F32)))
    idx = jnp.arange(chunk, dtype=F32)
    rel = idx[:, None] - idx[None, :]
    decay = jnp.where(rel >= 0, jnp.exp(log_gamma[:, None, None] * jnp.maximum(rel, 0.0)), 0.0)
    k_w = jnp.exp(log_gamma[:, None] * (chunk - 1.0 - idx)[None, :])
    q_w = jnp.exp(log_gamma[:, None] * (idx + 1.0)[None, :])
    per_lane = lambda t: jnp.repeat(t.T, RET_K_DIM, axis=1)
    chunk_decay = jnp.exp(log_gamma * chunk)
    cd = jnp.broadcast_to(jnp.repeat(chunk_decay, RET_K_DIM).reshape(N_HEADS // 2, LANES, 1),
                          (N_HEADS // 2, LANES, 2 * HEAD_DIM))
    r = jnp.arange(LANES)[:, None] // RET_K_DIM
    c = jnp.arange(2 * HEAD_DIM)[None, :] // HEAD_DIM
    bd = (r == c).astype(F32)
    return cos, sin, decay, per_lane(q_w), per_lane(k_w), cd, bd


def _retention(z_hi):
    bsz, seq, _ = z_hi.shape
    chunk = min(256, seq)
    cos, sin, decay, q_w, k_w, cd, bd = _retention_tables(seq, chunk)
    d_qk = N_HEADS * RET_K_DIM
    const = lambda shape: pl.BlockSpec(shape, lambda b, n: (0,) * len(shape))
    return pl.pallas_call(
        functools.partial(_ret_kernel, chunk=chunk),
        grid=(bsz, seq // chunk),
        in_specs=[pl.BlockSpec((1, chunk, d_qk), lambda b, n: (b, n, ZHI_RQ // d_qk)),
                  pl.BlockSpec((1, chunk, d_qk), lambda b, n: (b, n, ZHI_RK // d_qk)),
                  pl.BlockSpec((1, chunk, D_GROUP), lambda b, n: (b, n, ZHI_RV // D_GROUP)),
                  pl.BlockSpec((1, chunk, D_GROUP), lambda b, n: (b, n, ZHI_RG // D_GROUP)),
                  pl.BlockSpec((chunk, LANES), lambda b, n: (n, 0)),
                  pl.BlockSpec((chunk, LANES), lambda b, n: (n, 0)),
                  const((N_HEADS, chunk, chunk)), const((chunk, d_qk)), const((chunk, d_qk)),
                  const((N_HEADS // 2, LANES, 2 * HEAD_DIM)), const((LANES, 2 * HEAD_DIM))],
        out_specs=pl.BlockSpec((1, chunk, D_GROUP), lambda b, n: (b, n, 0)),
        out_shape=jax.ShapeDtypeStruct((bsz, seq, D_GROUP), BF16),
        scratch_shapes=[pltpu.VMEM((N_HEADS // 2, LANES, 2 * HEAD_DIM), F32)],
        compiler_params=_cparams(("arbitrary", "arbitrary"), 32),
        name="retention",
    )(z_hi, z_hi, z_hi, z_hi, cos, sin, decay, q_w, k_w, cd, bd)


def _gelu_tanh(x):
    return 0.5 * x * (1.0 + jnp.tanh(math.sqrt(2.0 / math.pi) * (x + 0.044715 * (x * x * x))))


def _sg_kernel(u_ref, v_ref, w_ref, bt_ref, lg_ref, lb_ref, o_ref, *, bs):
    u = _gelu_tanh(u_ref[0].astype(F32))
    v = _layer_norm_rows(_gelu_tanh(v_ref[0].astype(F32)), lg_ref[...], lb_ref[...]).astype(BF16)
    row = lax.broadcasted_iota(jnp.int32, (SG_CHUNK, SG_CHUNK), 0)
    col = lax.broadcasted_iota(jnp.int32, (SG_CHUNK, SG_CHUNK), 1)
    causal = col <= row
    for g in range(N_HEADS):
        cs = slice(g * HEAD_DIM, (g + 1) * HEAD_DIM)
        w_g = jnp.where(causal, w_ref[g], 0.0).astype(BF16)
        bias = bt_ref[:, g:g + 1]
        for r in range(bs // SG_CHUNK):
            rs = slice(r * SG_CHUNK, (r + 1) * SG_CHUNK)
            mixed = jnp.dot(w_g, v[rs, cs], preferred_element_type=F32) + bias
            o_ref[0, rs, cs] = (u[rs, cs] * mixed).astype(BF16)


def _spatial_gating(z_hi, sg_w, sg_b, ln_g, ln_b):
    bsz, seq, _ = z_hi.shape
    bs = min(512, seq)
    vec = lambda t: t.reshape(1, D_GROUP)
    tile = lambda off: pl.BlockSpec((1, bs, D_GROUP), lambda b, i: (b, i, off // D_GROUP))
    return pl.pallas_call(
        functools.partial(_sg_kernel, bs=bs),
        grid=(bsz, seq // bs),
        in_specs=[tile(ZHI_SGU), tile(ZHI_SGV),
                  pl.BlockSpec((N_HEADS, SG_CHUNK, SG_CHUNK), lambda b, i: (0, 0, 0)),
                  pl.BlockSpec((SG_CHUNK, N_HEADS), lambda b, i: (0, 0)),
                  pl.BlockSpec((1, D_GROUP), lambda b, i: (0, 0)),
                  pl.BlockSpec((1, D_GROUP), lambda b, i: (0, 0))],
        out_specs=pl.BlockSpec((1, bs, D_GROUP), lambda b, i: (b, i, 0)),
        out_shape=jax.ShapeDtypeStruct((bsz, seq, D_GROUP), BF16),
        compiler_params=_cparams(("arbitrary", "arbitrary"), 32),
        name="spatial_gating",
    )(z_hi, z_hi, sg_w, sg_b.T, vec(ln_g), vec(ln_b))


def _forget_weights(w_in_l, f_bias_l):
    d = w_in_l.shape[0]
    wf = w_in_l[:, FF_OFF:FF_OFF + N_HEADS]
    wf = jnp.concatenate([wf, wf, wf, jnp.zeros((d, LANES - 3 * N_HEADS), F32)], axis=1).astype(BF16)
    fb = jnp.concatenate([f_bias_l, f_bias_l, f_bias_l, jnp.zeros((LANES - 3 * N_HEADS,), F32)]).reshape(1, LANES)
    return wf, fb


def kernel(x, c, ada_w, ada_b, mix_pre_g, mix_post_g, w_in, fox_f_bias, conv_w, conv_b, conv_ln_g, conv_ln_b,
           sg_w, sg_b, sg_ln_g, sg_ln_b, w_out, ffn_pre_g, ffn_post_g, w_ff1, w_ff2):
    bsz, seq, d = x.shape
    n_layers = ada_w.shape[0]
    m = bsz * seq
    mod = _modulation(c, ada_w, ada_b).reshape(n_layers, bsz, 6, 1, d)
    shift1, scale1, gate1, shift2, scale2, gate2 = [mod[:, :, t] for t in range(6)]

    w_in_hi = w_in[:, :, FF_OFF + N_HEADS:]
    scale_lo = jnp.ones((D_ZHALF,), F32).at[ZLO_FQ:ZLO_FQ + D_GROUP].set(LOG2E * HEAD_DIM ** -0.5)
    scale_hi = jnp.ones((D_ZHALF,), F32).at[ZHI_RK:ZHI_RK + N_HEADS * RET_K_DIM].set(RET_K_DIM ** -0.5)
    w_ff2_b = w_ff2.astype(BF16)

    x2 = x.reshape(m, d)
    h, c3 = _rows(x2, seq, pre=(mix_pre_g[0], scale1[0], shift1[0]),
                  forget=_forget_weights(w_in[0], fox_f_bias[0]))
    for l in range(n_layers):
        z_lo = _matmul_ws([h], w_in, l, D_ZHALF, BF16, col_scale=scale_lo).reshape(bsz, seq, D_ZHALF)
        z_hi = _matmul_ws([h], w_in_hi, l, D_ZHALF, BF16, col_scale=scale_hi).reshape(bsz, seq, D_ZHALF)
        y_conv = _conv_module(z_lo, conv_w[l], conv_b[l], conv_ln_g[l], conv_ln_b[l])
        y_fox = _forgetting_attention(z_lo, c3)
        y_ret = _retention(z_hi)
        y_sg = _spatial_gating(z_hi, sg_w[l], sg_b[l], sg_ln_g[l], sg_ln_b[l])
        parts = [t.reshape(m, D_GROUP) for t in (y_conv, y_fox, y_ret, y_sg)]
        y = _matmul_ws(parts, w_out, l, d, F32)
        x2, h = _rows(x2, seq, post=(y, mix_post_g[l], gate1[l]), pre=(ffn_pre_g[l], scale2[l], shift2[l]))
        u = _matmul_ws([h], w_ff1, l, w_ff1.shape[2], BF16, relu2=True)
        y = _matmul_ksplit(u, w_ff2_b, l)
        if l + 1 < n_layers:
            x2, h, c3 = _rows(x2, seq, post=(y, ffn_post_g[l], gate2[l]),
                              pre=(mix_pre_g[l + 1], scale1[l + 1], shift1[l + 1]),
                              forget=_forget_weights(w_in[l + 1], fox_f_bias[l + 1]))
        else:
            (x2,) = _rows(x2, seq, post=(y, ffn_post_g[l], gate2[l]))
    return x2.reshape(bsz, seq, d)
```

```python
import functools
import math

import jax
import jax.numpy as jnp
from jax import lax
from jax.experimental import pallas as pl
from jax.experimental.pallas import tpu as pltpu

F32 = jnp.float32
BF16 = jnp.bfloat16

D_GROUP = 1024
N_HEADS = 8
HEAD_DIM = D_GROUP // N_HEADS
CONV_WIDTH = 31
CONV_HALO = 32
RET_K_DIM = HEAD_DIM // 2
SG_CHUNK = 128
ROPE_BASE = 10000.0
EPS = 1e-6
GN_EPS = 1e-5
LOG2E = math.log2(math.e)

D_ZHALF = 5120
FF_OFF = 5120
ZLO_GLU_A, ZLO_GLU_B, ZLO_FQ, ZLO_FK, ZLO_FV = 0, 1024, 2048, 3072, 4096
ZHI_RQ, ZHI_RK, ZHI_RV, ZHI_RG, ZHI_SGU, ZHI_SGV = 0, 512, 1024, 2048, 3072, 4096

MIB = 1024 * 1024
LANES = 128
SUBLANES = 8


def _cparams(semantics, vmem_mib):
    return pltpu.CompilerParams(dimension_semantics=semantics, vmem_limit_bytes=vmem_mib * MIB)


def _split3(x):
    p1 = x.astype(BF16)
    r1 = x - p1.astype(F32)
    p2 = r1.astype(BF16)
    p3 = (r1 - p2.astype(F32)).astype(BF16)
    return p1, p2, p3


def _mod_kernel(ct_ref, w_ref, b_ref, o_ref, sb_ref, *, bsz):
    @pl.when((pl.program_id(0) == 0) & (pl.program_id(1) == 0))
    def _():
        for b in range(bsz):
            cb = ct_ref[:, b:b + 1]
            sb_ref[b] = jnp.broadcast_to(cb * jax.nn.sigmoid(cb), sb_ref.shape[1:])

    o_ref[0] = jnp.zeros(o_ref.shape[1:], F32)
    tn = w_ref.shape[2]
    for cg in range(tn // LANES):
        cs = slice(cg * LANES, (cg + 1) * LANES)
        w = w_ref[0, :, cs]
        for b in range(bsz):
            o_ref[0, b:b + 1, cs] = jnp.sum(w * sb_ref[b], axis=0, keepdims=True) + b_ref[0, :, cs]


def _modulation(c, ada_w, ada_b):
    n_layers, d, n = ada_w.shape
    bsz = c.shape[0]
    rows = 8
    tn = 512
    out = pl.pallas_call(
        functools.partial(_mod_kernel, bsz=bsz),
        grid=(n_layers, n // tn),
        in_specs=[
            pl.BlockSpec((d, bsz), lambda l, j: (0, 0)),
            pl.BlockSpec((1, d, tn), lambda l, j: (l, 0, j)),
            pl.BlockSpec((1, 1, tn), lambda l, j: (l, 0, j)),
        ],
        out_specs=pl.BlockSpec((1, rows, tn), lambda l, j: (l, 0, j)),
        out_shape=jax.ShapeDtypeStruct((n_layers, rows, n), F32),
        scratch_shapes=[pltpu.VMEM((bsz, d, LANES), F32)],
        compiler_params=_cparams(("arbitrary", "arbitrary"), 40),
        name="adaln_mod",
    )(c.T, ada_w, ada_b.reshape(n_layers, 1, n))
    return out[:, :bsz]


def _log_sigmoid(x):
    return jnp.minimum(x, 0.0) - jnp.log(1.0 + jnp.exp(-jnp.abs(x)))


def _rows_kernel(*refs, has_post, has_pre, has_forget, tiles_per_seq):
    it = iter(refs)
    x_ref = next(it)
    if has_post:
        y_ref, gpost_ref, gate_ref = next(it), next(it), next(it)
    if has_pre:
        gpre_ref, sc_ref, sh_ref = next(it), next(it), next(it)
    if has_forget:
        wf_ref, fb_ref, tri_ref = next(it), next(it), next(it)
    if has_post:
        xo_ref = next(it)
    if has_pre:
        h_ref = next(it)
    if has_forget:
        c3_ref, carry_ref = next(it), next(it)

    x = x_ref[...]
    if has_post:
        y = y_ref[...]
        r = lax.rsqrt(jnp.mean(y * y, axis=-1, keepdims=True) + EPS)
        x = x + gate_ref[0] * (y * r * gpost_ref[...])
        xo_ref[...] = x
    if not has_pre:
        return
    r = lax.rsqrt(jnp.mean(x * x, axis=-1, keepdims=True) + EPS)
    hb = (x * r * (gpre_ref[...] * (1.0 + sc_ref[0])) + sh_ref[0]).astype(BF16)
    h_ref[...] = hb
    if not has_forget:
        return
    logf = _log_sigmoid(jnp.dot(hb, wf_ref[...], preferred_element_type=F32) + fb_ref[...])
    tri = tri_ref[...]
    local = sum(jnp.dot(tri, p, preferred_element_type=F32) for p in _split3(logf))

    @pl.when(pl.program_id(0) % tiles_per_seq == 0)
    def _():
        carry_ref[...] = jnp.zeros_like(carry_ref)

    cum = carry_ref[...] + local
    bm = cum.shape[0]
    carry_ref[...] = cum[bm - 1:bm, :]
    n1, n2, n3 = _split3(cum * (-LOG2E))
    lane = lax.broadcasted_iota(jnp.int32, cum.shape, 1)
    zero = jnp.zeros_like(n1)
    c3_ref[0] = jnp.where(lane < N_HEADS, n1,
                          jnp.where(lane < 2 * N_HEADS, n2, jnp.where(lane < 3 * N_HEADS, n3, zero)))


def _rows(x2, seq, post=None, pre=None, forget=None):
    m, d = x2.shape
    bsz = m // seq
    bm = min(256, seq)
    per_seq = seq // bm
    row = pl.BlockSpec((bm, d), lambda i: (i, 0))
    vec = pl.BlockSpec((1, d), lambda i: (0, 0))
    per_batch = pl.BlockSpec((1, 1, d), lambda i: (i // per_seq, 0, 0))
    args, in_specs, out_specs, out_shape, scratch = [x2], [row], [], [], []
    if post is not None:
        y, g_post, gate = post
        args += [y, g_post.reshape(1, d), gate]
        in_specs += [row, vec, per_batch]
    if pre is not None:
        g_pre, scale, shift = pre
        args += [g_pre.reshape(1, d), scale, shift]
        in_specs += [vec, per_batch, per_batch]
    if forget is not None:
        wf, fb = forget
        tri = (lax.broadcasted_iota(jnp.int32, (bm, bm), 1) <= lax.broadcasted_iota(jnp.int32, (bm, bm), 0)).astype(BF16)
        args += [wf, fb, tri]
        in_specs += [pl.BlockSpec((d, LANES), lambda i: (0, 0)), pl.BlockSpec((1, LANES), lambda i: (0, 0)),
                     pl.BlockSpec((bm, bm), lambda i: (0, 0))]
    if post is not None:
        out_specs.append(row)
        out_shape.append(jax.ShapeDtypeStruct((m, d), F32))
    if pre is not None:
        out_specs.append(row)
        out_shape.append(jax.ShapeDtypeStruct((m, d), BF16))
    if forget is not None:
        out_specs.append(pl.BlockSpec((1, bm, LANES), lambda i: (i // per_seq, i % per_seq, 0)))
        out_shape.append(jax.ShapeDtypeStruct((bsz, seq, LANES), BF16))
        scratch.append(pltpu.VMEM((1, LANES), F32))
    return pl.pallas_call(
        functools.partial(_rows_kernel, has_post=post is not None, has_pre=pre is not None,
                          has_forget=forget is not None, tiles_per_seq=per_seq),
        grid=(m // bm,),
        in_specs=in_specs, out_specs=out_specs, out_shape=out_shape, scratch_shapes=scratch,
        compiler_params=_cparams(("arbitrary",), 48),
        name="rows_norm",
    )(*args)


def _mm_kernel(*refs, n_a, relu2):
    a_refs, w_ref, o_ref = refs[:n_a], refs[n_a], refs[n_a + 1]
    ksz = w_ref.shape[1] // n_a
    acc = None
    for t, a_ref in enumerate(a_refs):
        part = jnp.dot(a_ref[...], w_ref[0, t * ksz:(t + 1) * ksz, :], preferred_element_type=F32)
        acc = part if acc is None else acc + part
    if relu2:
        acc = jnp.square(jnp.maximum(acc, 0.0))
    o_ref[...] = acc.astype(o_ref.dtype)


def _matmul(a_parts, w_all, layer, out_dtype, relu2=False):
    m = a_parts[0].shape[0]
    _, k, n = w_all.shape
    ka = k // len(a_parts)
    bm = min(1024, m)
    bn = min(1024, n)
    in_specs = [pl.BlockSpec((bm, ka), lambda i, j: (i, 0)) for _ in a_parts]
    in_specs.append(pl.BlockSpec((1, k, bn), lambda i, j: (layer, 0, j)))
    return pl.pallas_call(
        functools.partial(_mm_kernel, n_a=len(a_parts), relu2=relu2),
        grid=(m // bm, n // bn),
        in_specs=in_specs,
        out_specs=pl.BlockSpec((bm, bn), lambda i, j: (i, j)),
        out_shape=jax.ShapeDtypeStruct((m, n), out_dtype),
        compiler_params=_cparams(("arbitrary", "arbitrary"), 52),
        name="matmul_fullk",
    )(*a_parts, w_all)


def _mm_acc_kernel(a_ref, w_ref, o_ref):
    y = jnp.dot(a_ref[...], w_ref[0], preferred_element_type=F32)

    @pl.when(pl.program_id(2) == 0)
    def _():
        o_ref[...] = y

    @pl.when(pl.program_id(2) != 0)
    def _():
        o_ref[...] += y


def _matmul_ksplit(a, w_all, layer):
    m, k = a.shape
    n = w_all.shape[2]
    bm = min(1024, m)
    bn = min(1024, n)
    bk = min(4096, k)
    return pl.pallas_call(
        _mm_acc_kernel,
        grid=(m // bm, n // bn, k // bk),
        in_specs=[
            pl.BlockSpec((bm, bk), lambda i, j, q: (i, q)),
            pl.BlockSpec((1, bk, bn), lambda i, j, q: (layer, q, j)),
        ],
        out_specs=pl.BlockSpec((bm, bn), lambda i, j, q: (i, j)),
        out_shape=jax.ShapeDtypeStruct((m, n), F32),
        compiler_params=_cparams(("arbitrary", "arbitrary", "arbitrary"), 52),
        name="matmul_ksplit",
    )(a, w_all)


def _layer_norm_rows(h, g, b):
    mu = jnp.mean(h, axis=-1, keepdims=True)
    hc = h - mu
    var = jnp.mean(hc * hc, axis=-1, keepdims=True)
    return hc * lax.rsqrt(var + EPS) * g + b


def _conv_kernel(a_ref, b_ref, ah_ref, bh_ref, w_ref, cb_ref, lg_ref, lb_ref, o_ref, buf_ref, acc_ref, *, bs):
    i = pl.program_id(1)
    glu = a_ref[0].astype(F32) * jax.nn.sigmoid(b_ref[0].astype(F32))
    halo = ah_ref[0].astype(F32) * jax.nn.sigmoid(bh_ref[0].astype(F32))
    halo = jnp.where(i == 0, 0.0, halo)
    n_cb = D_GROUP // LANES
    for cb in range(n_cb):
        cs = slice(cb * LANES, (cb + 1) * LANES)
        buf_ref[cb, 0:CONV_HALO, :] = halo[:, cs]
        buf_ref[cb, CONV_HALO:CONV_HALO + bs, :] = glu[:, cs]
    first = CONV_HALO - (CONV_WIDTH - 1)
    n_u = bs // SUBLANES

    def per_col_block(cb, carry):
        taps = [jnp.broadcast_to(w_ref[cb, j:j + 1, :], (SUBLANES, LANES)) for j in range(CONV_WIDTH)]
        acc = [jnp.broadcast_to(cb_ref[cb], (SUBLANES, LANES)) for _ in range(n_u)]
        for r in range(SUBLANES):
            js = [j for j in range(CONV_WIDTH) if (first + j) % SUBLANES == r]
            ks = [(first + j) // SUBLANES for j in js]
            for v in range(min(ks), n_u + max(ks)):
                window = buf_ref[cb, r + SUBLANES * v:r + SUBLANES * (v + 1), :]
                for j, k in zip(js, ks):
                    if 0 <= v - k < n_u:
                        acc[v - k] = acc[v - k] + taps[j] * window
        for u in range(n_u):
            acc_ref[cb, u * SUBLANES:(u + 1) * SUBLANES, :] = acc[u]
        return carry

    lax.fori_loop(0, n_cb, per_col_block, 0)
    h = jnp.concatenate([acc_ref[cb] for cb in range(n_cb)], axis=1)
    h = _layer_norm_rows(h, lg_ref[...], lb_ref[...])
    o_ref[0] = (h * jax.nn.sigmoid(h)).astype(BF16)


def _conv_module(z_lo, conv_w, conv_b, ln_g, ln_b):
    bsz, seq, _ = z_lo.shape
    bs = min(128, seq)
    hb = bs // CONV_HALO
    vec = lambda v: v.reshape(1, D_GROUP)
    tile = lambda col: pl.BlockSpec((1, bs, D_GROUP), lambda b, i: (b, i, col))
    halo = lambda col: pl.BlockSpec((1, CONV_HALO, D_GROUP), lambda b, i: (b, jnp.maximum(i * hb - 1, 0), col))
    full = lambda r: pl.BlockSpec((r, D_GROUP), lambda b, i: (0, 0))
    n_cb = D_GROUP // LANES
    per_cb = lambda rows: pl.BlockSpec((n_cb, rows, LANES), lambda b, i: (0, 0, 0))
    w_cb = conv_w.reshape(CONV_WIDTH, n_cb, LANES).transpose(1, 0, 2)
    return pl.pallas_call(
        functools.partial(_conv_kernel, bs=bs),
        grid=(bsz, seq // bs),
        in_specs=[tile(ZLO_GLU_A // D_GROUP), tile(ZLO_GLU_B // D_GROUP),
                  halo(ZLO_GLU_A // D_GROUP), halo(ZLO_GLU_B // D_GROUP),
                  per_cb(CONV_WIDTH), per_cb(1), full(1), full(1)],
        out_specs=pl.BlockSpec((1, bs, D_GROUP), lambda b, i: (b, i, 0)),
        out_shape=jax.ShapeDtypeStruct((bsz, seq, D_GROUP), BF16),
        scratch_shapes=[pltpu.VMEM((n_cb, CONV_HALO + bs, LANES), F32),
                        pltpu.VMEM((n_cb, bs, LANES), F32)],
        compiler_params=_cparams(("arbitrary", "arbitrary"), 32),
        name="conv_module",
    )(z_lo, z_lo, z_lo, z_lo, w_cb, conv_b.reshape(n_cb, 1, LANES), vec(ln_g), vec(ln_b))


def _fox_kernel(q_ref, k_ref, v_ref, c3_ref, o_ref, ka_ref, va_ref, s0_ref, s1_ref, p_ref, m_ref, acc_ref,
                *, blk, sub, rg):
    h = pl.program_id(1)
    i = pl.program_id(2)

    @pl.when(i == 0)
    def _():
        ka_ref[:, :HEAD_DIM] = k_ref[0]
        ka_ref[:, HEAD_DIM:] = c3_ref[0]
        va_ref[:, :HEAD_DIM] = v_ref[0]
        lane = lax.broadcasted_iota(jnp.int32, (va_ref.shape[0], HEAD_DIM), 1)
        va_ref[:, HEAD_DIM:] = (lane == 0).astype(BF16)

    lane = lax.broadcasted_iota(jnp.int32, (blk, HEAD_DIM), 1)
    pick = (lane == h) | (lane == h + N_HEADS) | (lane == h + 2 * N_HEADS)
    qa = jnp.concatenate([q_ref[0], pick.astype(BF16)], axis=1)
    m_ref[...] = jnp.full(m_ref.shape, -jnp.inf, F32)
    acc_ref[...] = jnp.zeros(acc_ref.shape, F32)

    def logits_to(s_ref, jsub, r0=0):
        start = pl.multiple_of(jsub * sub, sub)
        s_ref[r0:, :] = lax.dot_general(qa[r0:], ka_ref[pl.ds(start, sub), :], (((1,), (1,)), ((), ())),
                                        preferred_element_type=F32)

    def update(s_ref, jsub, mask_off=None, r0=0):
        for r in range(r0 // rg, blk // rg):
            rows = slice(r * rg, (r + 1) * rg)
            s = s_ref[rows, :]
            if mask_off is not None and r * rg < mask_off + sub - 1:
                row = lax.broadcasted_iota(jnp.int32, (rg, sub), 0) + r * rg
                col = lax.broadcasted_iota(jnp.int32, (rg, sub), 1) + mask_off
                s = jnp.where(col <= row, s, -jnp.inf)
            m_old = m_ref[rows, :]
            m_new = jnp.maximum(m_old, jnp.max(s, axis=-1, keepdims=True))
            m_ref[rows, :] = m_new
            alpha = jnp.exp2(m_old - m_new)
            for cg in range(sub // LANES):
                cs = slice(cg * LANES, (cg + 1) * LANES)
                p_ref[rows, cs] = jnp.exp2(s[:, cs] - m_new).astype(BF16)
            for cg in range(acc_ref.shape[1] // LANES):
                cs = slice(cg * LANES, (cg + 1) * LANES)
                acc_ref[rows, cs] = alpha * acc_ref[rows, cs]
        start = pl.multiple_of(jsub * sub, sub)
        acc_ref[r0:, :] += jnp.dot(p_ref[r0:, :], va_ref[pl.ds(start, sub), :], preferred_element_type=F32)

    per = blk // sub
    logits_to(s0_ref, 0)

    def body(t, carry):
        logits_to(s1_ref, per * t + 1)
        update(s0_ref, per * t)
        logits_to(s0_ref, per * t + 2)
        update(s1_ref, per * t + 1)
        return carry

    lax.fori_loop(0, i, body, 0)
    logits_to(s1_ref, per * i + 1, r0=sub)
    update(s0_ref, per * i, mask_off=0)
    update(s1_ref, per * i + 1, mask_off=sub, r0=sub)
    acc = acc_ref[...]
    o_ref[0] = (acc[:, :HEAD_DIM] * (1.0 / acc[:, HEAD_DIM:HEAD_DIM + 1])).astype(BF16)


def _forgetting_attention(z_lo, c3):
    bsz, seq, _ = z_lo.shape
    blk = min(1024, seq)
    sub = blk // 2
    rg = min(128, blk)
    col = lambda off: off // HEAD_DIM
    kv = lambda off: pl.BlockSpec((1, seq, HEAD_DIM), lambda b, h, i: (b, 0, col(off) + h))
    return pl.pallas_call(
        functools.partial(_fox_kernel, blk=blk, sub=sub, rg=rg),
        grid=(bsz, N_HEADS, seq // blk),
        in_specs=[pl.BlockSpec((1, blk, HEAD_DIM), lambda b, h, i: (b, i, col(ZLO_FQ) + h)),
                  kv(ZLO_FK), kv(ZLO_FV),
                  pl.BlockSpec((1, seq, LANES), lambda b, h, i: (b, 0, 0))],
        out_specs=pl.BlockSpec((1, blk, HEAD_DIM), lambda b, h, i: (b, i, h)),
        out_shape=jax.ShapeDtypeStruct((bsz, seq, D_GROUP), BF16),
        scratch_shapes=[pltpu.VMEM((seq, 2 * HEAD_DIM), BF16), pltpu.VMEM((seq, 2 * HEAD_DIM), BF16),
                        pltpu.VMEM((blk, sub), F32), pltpu.VMEM((blk, sub), F32), pltpu.VMEM((blk, sub), BF16),
                        pltpu.VMEM((blk, LANES), F32),
                        pltpu.VMEM((blk, 2 * HEAD_DIM), F32)],
        compiler_params=_cparams(("arbitrary", "arbitrary", "arbitrary"), 48),
        name="forgetting_attention",
    )(z_lo, z_lo, z_lo, c3)


def _ret_kernel(q_ref, k_ref, v_ref, g_ref, cos_ref, sin_ref, dec_ref, qw_ref, kw_ref, cd_ref, bd_ref,
                o_ref, state_ref, *, chunk):
    n = pl.program_id(1)

    @pl.when(n == 0)
    def _():
        state_ref[...] = jnp.zeros_like(state_ref)

    cos = cos_ref[...]
    sin = sin_ref[...]
    lane = lax.broadcasted_iota(jnp.int32, (chunk, LANES), 1)
    first_half = (lane & (RET_K_DIM // 2)) == 0
    low_head = lane < RET_K_DIM

    def rotate(x):
        swapped = jnp.where(first_half, pltpu.roll(x, LANES - RET_K_DIM // 2, 1), pltpu.roll(x, RET_K_DIM // 2, 1))
        return x * cos + swapped * sin

    for pr in range(N_HEADS // 2):
        ls = slice(pr * LANES, (pr + 1) * LANES)
        qr = rotate(q_ref[0, :, ls].astype(F32))
        kr = rotate(k_ref[0, :, ls].astype(F32))
        kb = kr.astype(BF16)
        v_pair = v_ref[0, :, pr * 2 * HEAD_DIM:(pr + 1) * 2 * HEAD_DIM]
        state = state_ref[pr]
        cross = jnp.dot((qr * qw_ref[:, ls]).astype(BF16), state.astype(BF16), preferred_element_type=F32)
        for a in range(2):
            head = 2 * pr + a
            hs = slice(head * HEAD_DIM, (head + 1) * HEAD_DIM)
            qa = jnp.where(low_head if a == 0 else jnp.logical_not(low_head), qr, 0.0).astype(BF16)
            scores = lax.dot_general(qa, kb, (((1,), (1,)), ((), ())), preferred_element_type=F32) * dec_ref[head]
            inner = jnp.dot(scores.astype(BF16), v_pair[:, a * HEAD_DIM:(a + 1) * HEAD_DIM],
                            preferred_element_type=F32)
            y = inner + cross[:, a * HEAD_DIM:(a + 1) * HEAD_DIM]
            mu = jnp.mean(y, axis=-1, keepdims=True)
            yc = y - mu
            var = jnp.mean(yc * yc, axis=-1, keepdims=True)
            yn = yc * lax.rsqrt(var + GN_EPS)
            gate = g_ref[0, :, hs].astype(F32)
            o_ref[0, :, hs] = (gate * jax.nn.sigmoid(gate) * yn).astype(BF16)
        kw_t = (kr * kw_ref[:, ls]).T.astype(BF16)
        kv = jnp.dot(kw_t, v_pair, preferred_element_type=F32)
        state_ref[pr] = state * cd_ref[pr] + kv * bd_ref[...]


def _retention_tables(seq, chunk):
    half = RET_K_DIM // 2
    inv = 1.0 / (ROPE_BASE ** jnp.linspace(0.0, 1.0, half, dtype=F32))
    ang = jnp.arange(seq).astype(F32)[:, None] * inv[None, :]
    cos = jnp.tile(jnp.cos(ang), (1, LANES // half))
    sin = jnp.tile(jnp.concatenate([-jnp.sin(ang), jnp.sin(ang)], axis=-1), (1, LANES // RET_K_DIM))
    log_gamma = jnp.log(1.0 - 2.0 ** (-5.0 - jnp.arange(N_HEADS, dtype=F32)))
    idx = jnp.arange(chunk, dtype=F32)
    rel = idx[:, None] - idx[None, :]
    decay = jnp.where(rel >= 0, jnp.exp(log_gamma[:, None, None] * jnp.maximum(rel, 0.0)), 0.0)
    k_w = jnp.exp(log_gamma[:, None] * (chunk - 1.0 - idx)[None, :])
    q_w = jnp.exp(log_gamma[:, None] * (idx + 1.0)[None, :])
    per_lane = lambda t: jnp.repeat(t.T, RET_K_DIM, axis=1)
    chunk_decay = jnp.exp(log_gamma * chunk)
    cd = jnp.broadcast_to(jnp.repeat(chunk_decay, RET_K_DIM).reshape(N_HEADS // 2, LANES, 1),
                          (N_HEADS // 2, LANES, 2 * HEAD_DIM))
    r = jnp.arange(LANES)[:, None] // RET_K_DIM
    c = jnp.arange(2 * HEAD_DIM)[None, :] // HEAD_DIM
    bd = (r == c).astype(F32)
    return cos, sin, decay, per_lane(q_w), per_lane(k_w), cd, bd


def _retention(z_hi):
    bsz, seq, _ = z_hi.shape
    chunk = min(256, seq)
    cos, sin, decay, q_w, k_w, cd, bd = _retention_tables(seq, chunk)
    d_qk = N_HEADS * RET_K_DIM
    const = lambda shape: pl.BlockSpec(shape, lambda b, n: (0,) * len(shape))
    return pl.pallas_call(
        functools.partial(_ret_kernel, chunk=chunk),
        grid=(bsz, seq // chunk),
        in_specs=[pl.BlockSpec((1, chunk, d_qk), lambda b, n: (b, n, ZHI_RQ // d_qk)),
                  pl.BlockSpec((1, chunk, d_qk), lambda b, n: (b, n, ZHI_RK // d_qk)),
                  pl.BlockSpec((1, chunk, D_GROUP), lambda b, n: (b, n, ZHI_RV // D_GROUP)),
                  pl.BlockSpec((1, chunk, D_GROUP), lambda b, n: (b, n, ZHI_RG // D_GROUP)),
                  pl.BlockSpec((chunk, LANES), lambda b, n: (n, 0)),
                  pl.BlockSpec((chunk, LANES), lambda b, n: (n, 0)),
                  const((N_HEADS, chunk, chunk)), const((chunk, d_qk)), const((chunk, d_qk)),
                  const((N_HEADS // 2, LANES, 2 * HEAD_DIM)), const((LANES, 2 * HEAD_DIM))],
        out_specs=pl.BlockSpec((1, chunk, D_GROUP), lambda b, n: (b, n, 0)),
        out_shape=jax.ShapeDtypeStruct((bsz, seq, D_GROUP), BF16),
        scratch_shapes=[pltpu.VMEM((N_HEADS // 2, LANES, 2 * HEAD_DIM), F32)],
        compiler_params=_cparams(("arbitrary", "arbitrary"), 32),
        name="retention",
    )(z_hi, z_hi, z_hi, z_hi, cos, sin, decay, q_w, k_w, cd, bd)


def _gelu_tanh(x):
    return 0.5 * x * (1.0 + jnp.tanh(math.sqrt(2.0 / math.pi) * (x + 0.044715 * (x * x * x))))


def _sg_kernel(u_ref, v_ref, w_ref, bt_ref, lg_ref, lb_ref, o_ref, *, bs):
    u = _gelu_tanh(u_ref[0].astype(F32))
    v = _layer_norm_rows(_gelu_tanh(v_ref[0].astype(F32)), lg_ref[...], lb_ref[...]).astype(BF16)
    row = lax.broadcasted_iota(jnp.int32, (SG_CHUNK, SG_CHUNK), 0)
    col = lax.broadcasted_iota(jnp.int32, (SG_CHUNK, SG_CHUNK), 1)
    causal = col <= row
    for g in range(N_HEADS):
        cs = slice(g * HEAD_DIM, (g + 1) * HEAD_DIM)
        w_g = jnp.where(causal, w_ref[g], 0.0).astype(BF16)
        bias = bt_ref[:, g:g + 1]
        for r in range(bs // SG_CHUNK):
            rs = slice(r * SG_CHUNK, (r + 1) * SG_CHUNK)
            mixed = jnp.dot(w_g, v[rs, cs], preferred_element_type=F32) + bias
            o_ref[0, rs, cs] = (u[rs, cs] * mixed).astype(BF16)


def _spatial_gating(z_hi, sg_w, sg_b, ln_g, ln_b):
    bsz, seq, _ = z_hi.shape
    bs = min(512, seq)
    vec = lambda t: t.reshape(1, D_GROUP)
    tile = lambda off: pl.BlockSpec((1, bs, D_GROUP), lambda b, i: (b, i, off // D_GROUP))
    return pl.pallas_call(
        functools.partial(_sg_kernel, bs=bs),
        grid=(bsz, seq // bs),
        in_specs=[tile(ZHI_SGU), tile(ZHI_SGV),
                  pl.BlockSpec((N_HEADS, SG_CHUNK, SG_CHUNK), lambda b, i: (0, 0, 0)),
                  pl.BlockSpec((SG_CHUNK, N_HEADS), lambda b, i: (0, 0)),
                  pl.BlockSpec((1, D_GROUP), lambda b, i: (0, 0)),
                  pl.BlockSpec((1, D_GROUP), lambda b, i: (0, 0))],
        out_specs=pl.BlockSpec((1, bs, D_GROUP), lambda b, i: (b, i, 0)),
        out_shape=jax.ShapeDtypeStruct((bsz, seq, D_GROUP), BF16),
        compiler_params=_cparams(("arbitrary", "arbitrary"), 32),
        name="spatial_gating",
    )(z_hi, z_hi, sg_w, sg_b.T, vec(ln_g), vec(ln_b))


def _forget_weights(w_gate_l, f_bias_l):
    d = w_gate_l.shape[0]
    wf = jnp.concatenate([w_gate_l, w_gate_l, w_gate_l, jnp.zeros((d, LANES - 3 * N_HEADS), F32)], axis=1).astype(BF16)
    fb = jnp.concatenate([f_bias_l, f_bias_l, f_bias_l, jnp.zeros((LANES - 3 * N_HEADS,), F32)]).reshape(1, LANES)
    return wf, fb


def kernel(x, c, ada_w, ada_b, mix_pre_g, mix_post_g, w_in, fox_f_bias, conv_w, conv_b, conv_ln_g, conv_ln_b,
           sg_w, sg_b, sg_ln_g, sg_ln_b, w_out, ffn_pre_g, ffn_post_g, w_ff1, w_ff2):
    bsz, seq, d = x.shape
    n_layers = ada_w.shape[0]
    m = bsz * seq
    mod = _modulation(c, ada_w, ada_b).reshape(n_layers, bsz, 6, 1, d)
    shift1, scale1, gate1, shift2, scale2, gate2 = [mod[:, :, t] for t in range(6)]

    scale_lo = jnp.ones((D_ZHALF,), F32).at[ZLO_FQ:ZLO_FQ + D_GROUP].set(LOG2E * HEAD_DIM ** -0.5)
    scale_hi = jnp.ones((D_ZHALF,), F32).at[ZHI_RK:ZHI_RK + N_HEADS * RET_K_DIM].set(RET_K_DIM ** -0.5)
    w_lo_b = (w_in[:, :, :FF_OFF] * scale_lo).astype(BF16)
    w_hi_b = (w_in[:, :, FF_OFF + N_HEADS:] * scale_hi).astype(BF16)
    w_gate = w_in[:, :, FF_OFF:FF_OFF + N_HEADS]
    w_out_b = w_out.astype(BF16)
    w_ff1_b = w_ff1.astype(BF16)
    w_ff2_b = w_ff2.astype(BF16)

    x2 = x.reshape(m, d)
    h, c3 = _rows(x2, seq, pre=(mix_pre_g[0], scale1[0], shift1[0]),
                  forget=_forget_weights(w_gate[0], fox_f_bias[0]))
    for l in range(n_layers):
        z_lo = _matmul([h], w_lo_b, l, BF16).reshape(bsz, seq, D_ZHALF)
        z_hi = _matmul([h], w_hi_b, l, BF16).reshape(bsz, seq, D_ZHALF)
        y_conv = _conv_module(z_lo, conv_w[l], conv_b[l], conv_ln_g[l], conv_ln_b[l])
        y_fox = _forgetting_attention(z_lo, c3)
        y_ret = _retention(z_hi)
        y_sg = _spatial_gating(z_hi, sg_w[l], sg_b[l], sg_ln_g[l], sg_ln_b[l])
        parts = [t.reshape(m, D_GROUP) for t in (y_conv, y_fox, y_ret, y_sg)]
        y = _matmul(parts, w_out_b, l, F32)
        x2, h = _rows(x2, seq, post=(y, mix_post_g[l], gate1[l]), pre=(ffn_pre_g[l], scale2[l], shift2[l]))
        u = _matmul([h], w_ff1_b, l, BF16, relu2=True)
        y = _matmul_ksplit(u, w_ff2_b, l)
        if l + 1 < n_layers:
            x2, h, c3 = _rows(x2, seq, post=(y, ffn_post_g[l], gate2[l]),
                              pre=(mix_pre_g[l + 1], scale1[l + 1], shift1[l + 1]),
                              forget=_forget_weights(w_gate[l + 1], fox_f_bias[l + 1]))
        else:
            (x2,) = _rows(x2, seq, post=(y, ffn_post_g[l], gate2[l]))
    return x2.reshape(bsz, seq, d)
```

```python
import functools
import math

import jax
import jax.numpy as jnp
from jax import lax
from jax.experimental import pallas as pl
from jax.experimental.pallas import tpu as pltpu

F32 = jnp.float32
BF16 = jnp.bfloat16

D_GROUP = 1024
N_HEADS = 8
HEAD_DIM = D_GROUP // N_HEADS
CONV_WIDTH = 31
CONV_HALO = 32
RET_K_DIM = HEAD_DIM // 2
SG_CHUNK = 128
ROPE_BASE = 10000.0
EPS = 1e-6
GN_EPS = 1e-5
LOG2E = math.log2(math.e)
FOX_SCALE = LOG2E * HEAD_DIM ** -0.5

D_ZHALF = 5120
FF_OFF = 5120
ZLO_GLU_A, ZLO_GLU_B, ZLO_FQ, ZLO_FK, ZLO_FV = 0, 1024, 2048, 3072, 4096
ZHI_RQ, ZHI_RK, ZHI_RV, ZHI_RG, ZHI_SGU, ZHI_SGV = 0, 512, 1024, 2048, 3072, 4096

MIB = 1024 * 1024
LANES = 128
SUBLANES = 8


def _cparams(semantics, vmem_mib):
    return pltpu.CompilerParams(dimension_semantics=semantics, vmem_limit_bytes=vmem_mib * MIB)


def _split3(x):
    p1 = x.astype(BF16)
    r1 = x - p1.astype(F32)
    p2 = r1.astype(BF16)
    p3 = (r1 - p2.astype(F32)).astype(BF16)
    return p1, p2, p3


def _mod_kernel(ct_ref, w_ref, b_ref, o_ref, sb_ref, acc_ref, *, bsz):
    q = pl.program_id(1)

    @pl.when((pl.program_id(0) == 0) & (q == 0))
    def _():
        for b in range(bsz):
            cb = ct_ref[:, b:b + 1]
            sb_ref[b] = jnp.broadcast_to(cb * jax.nn.sigmoid(cb), sb_ref.shape[1:])

    @pl.when(q == 0)
    def _():
        acc_ref[...] = jnp.zeros_like(acc_ref)

    kc, n = w_ref.shape[1], w_ref.shape[2]
    start = pl.multiple_of(q * kc, kc)
    s_rows = [sb_ref[b, pl.ds(start, kc), :] for b in range(bsz)]
    for cg in range(n // LANES):
        cs = slice(cg * LANES, (cg + 1) * LANES)
        w = w_ref[0, :, cs]
        for b in range(bsz):
            acc_ref[b, :, cs] += jnp.sum((w * s_rows[b]).reshape(kc // SUBLANES, SUBLANES, LANES), axis=0)

    @pl.when(q == pl.num_programs(1) - 1)
    def _():
        o_ref[0] = jnp.zeros(o_ref.shape[1:], F32)
        for b in range(bsz):
            o_ref[0, b:b + 1, :] = jnp.sum(acc_ref[b], axis=0, keepdims=True) + b_ref[0]


def _modulation(c, ada_w, ada_b):
    n_layers, d, n = ada_w.shape
    bsz = c.shape[0]
    rows = SUBLANES
    kc = 128
    out = pl.pallas_call(
        functools.partial(_mod_kernel, bsz=bsz),
        grid=(n_layers, d // kc),
        in_specs=[
            pl.BlockSpec((d, bsz), lambda l, q: (0, 0)),
            pl.BlockSpec((1, kc, n), lambda l, q: (l, q, 0)),
            pl.BlockSpec((1, 1, n), lambda l, q: (l, 0, 0)),
        ],
        out_specs=pl.BlockSpec((1, rows, n), lambda l, q: (l, 0, 0)),
        out_shape=jax.ShapeDtypeStruct((n_layers, rows, n), F32),
        scratch_shapes=[pltpu.VMEM((bsz, d, LANES), F32), pltpu.VMEM((bsz, SUBLANES, n), F32)],
        compiler_params=_cparams(("arbitrary", "arbitrary"), 48),
        name="adaln_mod",
    )(c.T, ada_w, ada_b.reshape(n_layers, 1, n))
    return out[:, :bsz]


def _log_sigmoid(x):
    return jnp.minimum(x, 0.0) - jnp.log(1.0 + jnp.exp(-jnp.abs(x)))


def _rows_kernel(*refs, has_post, has_pre, has_forget, tiles_per_seq):
    it = iter(refs)
    x_ref = next(it)
    if has_post:
        y_ref, gpost_ref, gate_ref = next(it), next(it), next(it)
    if has_pre:
        gpre_ref, sc_ref, sh_ref = next(it), next(it), next(it)
    if has_forget:
        wf_ref, fb_ref, tri_ref = next(it), next(it), next(it)
    if has_post:
        xo_ref = next(it)
    if has_pre:
        h_ref = next(it)
    if has_forget:
        c3_ref, carry_ref = next(it), next(it)

    x = x_ref[...]
    if has_post:
        y = y_ref[...]
        r = lax.rsqrt(jnp.mean(y * y, axis=-1, keepdims=True) + EPS)
        x = x + gate_ref[0] * (y * r * gpost_ref[...])
        xo_ref[...] = x
    if not has_pre:
        return
    r = lax.rsqrt(jnp.mean(x * x, axis=-1, keepdims=True) + EPS)
    hb = (x * r * (gpre_ref[...] * (1.0 + sc_ref[0])) + sh_ref[0]).astype(BF16)
    h_ref[...] = hb
    if not has_forget:
        return
    logf = _log_sigmoid(jnp.dot(hb, wf_ref[...], preferred_element_type=F32) + fb_ref[...])
    tri = tri_ref[...]
    local = sum(jnp.dot(tri, p, preferred_element_type=F32) for p in _split3(logf))

    @pl.when(pl.program_id(0) % tiles_per_seq == 0)
    def _():
        carry_ref[...] = jnp.zeros_like(carry_ref)

    cum = carry_ref[...] + local
    bm = cum.shape[0]
    carry_ref[...] = cum[bm - 1:bm, :]
    n1, n2, n3 = _split3(cum * (-LOG2E))
    lane = lax.broadcasted_iota(jnp.int32, cum.shape, 1)
    zero = jnp.zeros_like(n1)
    c3_ref[0] = jnp.where(lane < N_HEADS, n1,
                          jnp.where(lane < 2 * N_HEADS, n2, jnp.where(lane < 3 * N_HEADS, n3, zero)))


def _rows(x2, seq, post=None, pre=None, forget=None):
    m, d = x2.shape
    bsz = m // seq
    bm = min(256, seq)
    per_seq = seq // bm
    row = pl.BlockSpec((bm, d), lambda i: (i, 0))
    vec = pl.BlockSpec((1, d), lambda i: (0, 0))
    per_batch = pl.BlockSpec((1, 1, d), lambda i: (i // per_seq, 0, 0))
    args, in_specs, out_specs, out_shape, scratch = [x2], [row], [], [], []
    if post is not None:
        y, g_post, gate = post
        args += [y, g_post.reshape(1, d), gate]
        in_specs += [row, vec, per_batch]
    if pre is not None:
        g_pre, scale, shift = pre
        args += [g_pre.reshape(1, d), scale, shift]
        in_specs += [vec, per_batch, per_batch]
    if forget is not None:
        wf, fb = forget
        tri = (lax.broadcasted_iota(jnp.int32, (bm, bm), 1) <= lax.broadcasted_iota(jnp.int32, (bm, bm), 0)).astype(BF16)
        args += [wf, fb, tri]
        in_specs += [pl.BlockSpec((d, LANES), lambda i: (0, 0)), pl.BlockSpec((1, LANES), lambda i: (0, 0)),
                     pl.BlockSpec((bm, bm), lambda i: (0, 0))]
    if post is not None:
        out_specs.append(row)
        out_shape.append(jax.ShapeDtypeStruct((m, d), F32))
    if pre is not None:
        out_specs.append(row)
        out_shape.append(jax.ShapeDtypeStruct((m, d), BF16))
    if forget is not None:
        out_specs.append(pl.BlockSpec((1, bm, LANES), lambda i: (i // per_seq, i % per_seq, 0)))
        out_shape.append(jax.ShapeDtypeStruct((bsz, seq, LANES), BF16))
        scratch.append(pltpu.VMEM((1, LANES), F32))
    return pl.pallas_call(
        functools.partial(_rows_kernel, has_post=post is not None, has_pre=pre is not None,
                          has_forget=forget is not None, tiles_per_seq=per_seq),
        grid=(m // bm,),
        in_specs=in_specs, out_specs=out_specs, out_shape=out_shape, scratch_shapes=scratch,
        compiler_params=_cparams(("arbitrary",), 48),
        name="rows_norm",
    )(*args)


def _mm_kernel(*refs, n_a, relu2, has_side):
    a_refs, w_ref = refs[:n_a], refs[n_a]
    if has_side:
        side_in_ref, o_ref, side_out_ref = refs[n_a + 1:n_a + 4]
        side_out_ref[...] = side_in_ref[...].astype(BF16)
    else:
        o_ref = refs[n_a + 1]
    ksz = w_ref.shape[1] // n_a
    acc = None
    for t, a_ref in enumerate(a_refs):
        part = jnp.dot(a_ref[...], w_ref[0, t * ksz:(t + 1) * ksz, :], preferred_element_type=F32)
        acc = part if acc is None else acc + part
    if relu2:
        acc = jnp.square(jnp.maximum(acc, 0.0))
    o_ref[...] = acc.astype(o_ref.dtype)


def _matmul(a_parts, w_all, layer, out_dtype, relu2=False, side=None):
    m = a_parts[0].shape[0]
    _, k, n = w_all.shape
    ka = k // len(a_parts)
    bm = min(1024, m)
    bn = min(1024, n)
    nn = n // bn
    in_specs = [pl.BlockSpec((bm, ka), lambda i, j: (i, 0)) for _ in a_parts]
    in_specs.append(pl.BlockSpec((1, k, bn), lambda i, j: (layer, 0, j)))
    out_specs = pl.BlockSpec((bm, bn), lambda i, j: (i, j))
    out_shape = jax.ShapeDtypeStruct((m, n), out_dtype)
    args = list(a_parts) + [w_all]
    if side is not None:
        src, src_layer = side
        _, rows, cols = src.shape
        n_slabs = 1
        while n_slabs * 2 <= (m // bm) * nn and rows % (n_slabs * 2) == 0 and rows // (n_slabs * 2) >= 2 * SUBLANES:
            n_slabs *= 2
        slab = rows // n_slabs
        slab_idx = lambda i, j: jnp.minimum(i * nn + j, n_slabs - 1)
        in_specs.append(pl.BlockSpec((1, slab, cols), lambda i, j: (src_layer, slab_idx(i, j), 0)))
        out_specs = [out_specs, pl.BlockSpec((1, slab, cols), lambda i, j: (0, slab_idx(i, j), 0))]
        out_shape = [out_shape, jax.ShapeDtypeStruct((1, rows, cols), BF16)]
        args.append(src)
    return pl.pallas_call(
        functools.partial(_mm_kernel, n_a=len(a_parts), relu2=relu2, has_side=side is not None),
        grid=(m // bm, nn),
        in_specs=in_specs,
        out_specs=out_specs,
        out_shape=out_shape,
        compiler_params=_cparams(("arbitrary", "arbitrary"), 56),
        name="matmul_fullk",
    )(*args)


def _mm_acc_kernel(a_ref, w_ref, o_ref):
    y = jnp.dot(a_ref[...], w_ref[0], preferred_element_type=F32)

    @pl.when(pl.program_id(2) == 0)
    def _():
        o_ref[...] = y

    @pl.when(pl.program_id(2) != 0)
    def _():
        o_ref[...] += y


def _matmul_ksplit(a, w_all, layer):
    m, k = a.shape
    n = w_all.shape[2]
    bm = min(1024, m)
    bn = min(1024, n)
    bk = min(4096, k)
    return pl.pallas_call(
        _mm_acc_kernel,
        grid=(m // bm, n // bn, k // bk),
        in_specs=[
            pl.BlockSpec((bm, bk), lambda i, j, q: (i, q)),
            pl.BlockSpec((1, bk, bn), lambda i, j, q: (layer, q, j)),
        ],
        out_specs=pl.BlockSpec((bm, bn), lambda i, j, q: (i, j)),
        out_shape=jax.ShapeDtypeStruct((m, n), F32),
        compiler_params=_cparams(("arbitrary", "arbitrary", "arbitrary"), 52),
        name="matmul_ksplit",
    )(a, w_all)


def _layer_norm_rows(h, g, b):
    mu = jnp.mean(h, axis=-1, keepdims=True)
    hc = h - mu
    var = jnp.mean(hc * hc, axis=-1, keepdims=True)
    return hc * lax.rsqrt(var + EPS) * g + b


def _conv_kernel(a_ref, b_ref, ah_ref, bh_ref, w_ref, cb_ref, lg_ref, lb_ref, o_ref, buf_ref, acc_ref, *, bs):
    i = pl.program_id(1)
    glu = a_ref[0].astype(F32) * jax.nn.sigmoid(b_ref[0].astype(F32))
    halo = ah_ref[0].astype(F32) * jax.nn.sigmoid(bh_ref[0].astype(F32))
    halo = jnp.where(i == 0, 0.0, halo)
    n_cb = D_GROUP // LANES
    for cb in range(n_cb):
        cs = slice(cb * LANES, (cb + 1) * LANES)
        buf_ref[cb, 0:CONV_HALO, :] = halo[:, cs]
        buf_ref[cb, CONV_HALO:CONV_HALO + bs, :] = glu[:, cs]
    first = CONV_HALO - (CONV_WIDTH - 1)
    n_u = bs // SUBLANES

    def per_col_block(cb, carry):
        taps = [jnp.broadcast_to(w_ref[cb, j:j + 1, :], (SUBLANES, LANES)) for j in range(CONV_WIDTH)]
        acc = [jnp.broadcast_to(cb_ref[cb], (SUBLANES, LANES)) for _ in range(n_u)]
        for r in range(SUBLANES):
            js = [j for j in range(CONV_WIDTH) if (first + j) % SUBLANES == r]
            ks = [(first + j) // SUBLANES for j in js]
            for v in range(min(ks), n_u + max(ks)):
                window = buf_ref[cb, r + SUBLANES * v:r + SUBLANES * (v + 1), :]
                for j, k in zip(js, ks):
                    if 0 <= v - k < n_u:
                        acc[v - k] = acc[v - k] + taps[j] * window
        for u in range(n_u):
            acc_ref[cb, u * SUBLANES:(u + 1) * SUBLANES, :] = acc[u]
        return carry

    lax.fori_loop(0, n_cb, per_col_block, 0)
    h = jnp.concatenate([acc_ref[cb] for cb in range(n_cb)], axis=1)
    h = _layer_norm_rows(h, lg_ref[...], lb_ref[...])
    o_ref[0] = (h * jax.nn.sigmoid(h)).astype(BF16)


def _conv_module(z_lo, conv_w, conv_b, ln_g, ln_b):
    bsz, seq, _ = z_lo.shape
    bs = min(128, seq)
    hb = bs // CONV_HALO
    vec = lambda v: v.reshape(1, D_GROUP)
    tile = lambda col: pl.BlockSpec((1, bs, D_GROUP), lambda b, i: (b, i, col))
    halo = lambda col: pl.BlockSpec((1, CONV_HALO, D_GROUP), lambda b, i: (b, jnp.maximum(i * hb - 1, 0), col))
    full = lambda r: pl.BlockSpec((r, D_GROUP), lambda b, i: (0, 0))
    n_cb = D_GROUP // LANES
    per_cb = lambda rows: pl.BlockSpec((n_cb, rows, LANES), lambda b, i: (0, 0, 0))
    w_cb = conv_w.reshape(CONV_WIDTH, n_cb, LANES).transpose(1, 0, 2)
    return pl.pallas_call(
        functools.partial(_conv_kernel, bs=bs),
        grid=(bsz, seq // bs),
        in_specs=[tile(ZLO_GLU_A // D_GROUP), tile(ZLO_GLU_B // D_GROUP),
                  halo(ZLO_GLU_A // D_GROUP), halo(ZLO_GLU_B // D_GROUP),
                  per_cb(CONV_WIDTH), per_cb(1), full(1), full(1)],
        out_specs=pl.BlockSpec((1, bs, D_GROUP), lambda b, i: (b, i, 0)),
        out_shape=jax.ShapeDtypeStruct((bsz, seq, D_GROUP), BF16),
        scratch_shapes=[pltpu.VMEM((n_cb, CONV_HALO + bs, LANES), F32),
                        pltpu.VMEM((n_cb, bs, LANES), F32)],
        compiler_params=_cparams(("arbitrary", "arbitrary"), 32),
        name="conv_module",
    )(z_lo, z_lo, z_lo, z_lo, w_cb, conv_b.reshape(n_cb, 1, LANES), vec(ln_g), vec(ln_b))


def _fox_kernel(q_ref, k_ref, v_ref, c3_ref, o_ref, ka_ref, va_ref, s0_ref, s1_ref, p_ref, m_ref, acc_ref,
                *, blk, sub, rg):
    h = pl.program_id(1)
    i = pl.program_id(2)

    @pl.when(i == 0)
    def _():
        ka_ref[:, :HEAD_DIM] = k_ref[0]
        ka_ref[:, HEAD_DIM:] = c3_ref[0]
        va_ref[:, :HEAD_DIM] = v_ref[0]
        lane = lax.broadcasted_iota(jnp.int32, (va_ref.shape[0], HEAD_DIM), 1)
        va_ref[:, HEAD_DIM:] = (lane == 0).astype(BF16)

    lane = lax.broadcasted_iota(jnp.int32, (blk, HEAD_DIM), 1)
    pick = (lane == h) | (lane == h + N_HEADS) | (lane == h + 2 * N_HEADS)
    qa = jnp.concatenate([q_ref[0], pick.astype(BF16)], axis=1)
    m_ref[...] = jnp.full(m_ref.shape, -jnp.inf, F32)
    acc_ref[...] = jnp.zeros(acc_ref.shape, F32)

    def logits_to(s_ref, jsub, r0=0):
        start = pl.multiple_of(jsub * sub, sub)
        s_ref[r0:, :] = lax.dot_general(qa[r0:], ka_ref[pl.ds(start, sub), :], (((1,), (1,)), ((), ())),
                                        preferred_element_type=F32)

    def update(s_ref, jsub, mask_off=None, r0=0):
        for r in range(r0 // rg, blk // rg):
            rows = slice(r * rg, (r + 1) * rg)
            s = s_ref[rows, :]
            if mask_off is not None and r * rg < mask_off + sub - 1:
                row = lax.broadcasted_iota(jnp.int32, (rg, sub), 0) + r * rg
                col = lax.broadcasted_iota(jnp.int32, (rg, sub), 1) + mask_off
                s = jnp.where(col <= row, s, -jnp.inf)
            m_old = m_ref[rows, :]
            m_new = jnp.maximum(m_old, jnp.max(s, axis=-1, keepdims=True))
            m_ref[rows, :] = m_new
            alpha = jnp.exp2(m_old - m_new)
            for cg in range(sub // LANES):
                cs = slice(cg * LANES, (cg + 1) * LANES)
                p_ref[rows, cs] = jnp.exp2(s[:, cs] - m_new).astype(BF16)
            for cg in range(acc_ref.shape[1] // LANES):
                cs = slice(cg * LANES, (cg + 1) * LANES)
                acc_ref[rows, cs] = alpha * acc_ref[rows, cs]
        start = pl.multiple_of(jsub * sub, sub)
        acc_ref[r0:, :] += jnp.dot(p_ref[r0:, :], va_ref[pl.ds(start, sub), :], preferred_element_type=F32)

    per = blk // sub
    logits_to(s0_ref, 0)

    def body(t, carry):
        logits_to(s1_ref, per * t + 1)
        update(s0_ref, per * t)
        logits_to(s0_ref, per * t + 2)
        update(s1_ref, per * t + 1)
        return carry

    lax.fori_loop(0, i, body, 0)
    logits_to(s1_ref, per * i + 1, r0=sub)
    update(s0_ref, per * i, mask_off=0)
    update(s1_ref, per * i + 1, mask_off=sub, r0=sub)
    acc = acc_ref[...]
    o_ref[0] = (acc[:, :HEAD_DIM] * (1.0 / acc[:, HEAD_DIM:HEAD_DIM + 1])).astype(BF16)


def _forgetting_attention(z_lo, c3):
    bsz, seq, _ = z_lo.shape
    blk = min(1024, seq)
    sub = blk // 2
    rg = min(128, blk)
    col = lambda off: off // HEAD_DIM
    kv = lambda off: pl.BlockSpec((1, seq, HEAD_DIM), lambda b, h, i: (b, 0, col(off) + h))
    return pl.pallas_call(
        functools.partial(_fox_kernel, blk=blk, sub=sub, rg=rg),
        grid=(bsz, N_HEADS, seq // blk),
        in_specs=[pl.BlockSpec((1, blk, HEAD_DIM), lambda b, h, i: (b, i, col(ZLO_FQ) + h)),
                  kv(ZLO_FK), kv(ZLO_FV),
                  pl.BlockSpec((1, seq, LANES), lambda b, h, i: (b, 0, 0))],
        out_specs=pl.BlockSpec((1, blk, HEAD_DIM), lambda b, h, i: (b, i, h)),
        out_shape=jax.ShapeDtypeStruct((bsz, seq, D_GROUP), BF16),
        scratch_shapes=[pltpu.VMEM((seq, 2 * HEAD_DIM), BF16), pltpu.VMEM((seq, 2 * HEAD_DIM), BF16),
                        pltpu.VMEM((blk, sub), F32), pltpu.VMEM((blk, sub), F32), pltpu.VMEM((blk, sub), BF16),
                        pltpu.VMEM((blk, LANES), F32),
                        pltpu.VMEM((blk, 2 * HEAD_DIM), F32)],
        compiler_params=_cparams(("arbitrary", "arbitrary", "arbitrary"), 48),
        name="forgetting_attention",
    )(z_lo, z_lo, z_lo, c3)


def _ret_kernel(q_ref, k_ref, v_ref, g_ref, cos_ref, sin_ref, dec_ref, qw_ref, kw_ref, cd_ref, bd_ref,
                o_ref, state_ref, *, chunk):
    n = pl.program_id(1)

    @pl.when(n == 0)
    def _():
        state_ref[...] = jnp.zeros_like(state_ref)

    cos = cos_ref[...]
    sin = sin_ref[...]
    lane = lax.broadcasted_iota(jnp.int32, (chunk, LANES), 1)
    first_half = (lane & (RET_K_DIM // 2)) == 0
    low_head = lane < RET_K_DIM

    def rotate(x):
        swapped = jnp.where(first_half, pltpu.roll(x, LANES - RET_K_DIM // 2, 1), pltpu.roll(x, RET_K_DIM // 2, 1))
        return x * cos + swapped * sin

    for pr in range(N_HEADS // 2):
        ls = slice(pr * LANES, (pr + 1) * LANES)
        qr = rotate(q_ref[0, :, ls].astype(F32))
        kr = rotate(k_ref[0, :, ls].astype(F32))
        kb = kr.astype(BF16)
        v_pair = v_ref[0, :, pr * 2 * HEAD_DIM:(pr + 1) * 2 * HEAD_DIM]
        state = state_ref[pr]
        cross = jnp.dot((qr * qw_ref[:, ls]).astype(BF16), state.astype(BF16), preferred_element_type=F32)
        for a in range(2):
            head = 2 * pr + a
            hs = slice(head * HEAD_DIM, (head + 1) * HEAD_DIM)
            qa = jnp.where(low_head if a == 0 else jnp.logical_not(low_head), qr, 0.0).astype(BF16)
            scores = lax.dot_general(qa, kb, (((1,), (1,)), ((), ())), preferred_element_type=F32) * dec_ref[head]
            inner = jnp.dot(scores.astype(BF16), v_pair[:, a * HEAD_DIM:(a + 1) * HEAD_DIM],
                            preferred_element_type=F32)
            y = inner + cross[:, a * HEAD_DIM:(a + 1) * HEAD_DIM]
            mu = jnp.mean(y, axis=-1, keepdims=True)
            yc = y - mu
            var = jnp.mean(yc * yc, axis=-1, keepdims=True)
            yn = yc * lax.rsqrt(var + GN_EPS)
            gate = g_ref[0, :, hs].astype(F32)
            o_ref[0, :, hs] = (gate * jax.nn.sigmoid(gate) * yn).astype(BF16)
        kw_t = (kr * kw_ref[:, ls]).T.astype(BF16)
        kv = jnp.dot(kw_t, v_pair, preferred_element_type=F32)
        state_ref[pr] = state * cd_ref[pr] + kv * bd_ref[...]


def _retention_tables(seq, chunk):
    half = RET_K_DIM // 2
    inv = 1.0 / (ROPE_BASE ** jnp.linspace(0.0, 1.0, half, dtype=F32))
    ang = jnp.arange(seq).astype(F32)[:, None] * inv[None, :]
    cos = jnp.tile(jnp.cos(ang), (1, LANES // half))
    sin = jnp.tile(jnp.concatenate([-jnp.sin(ang), jnp.sin(ang)], axis=-1), (1, LANES // RET_K_DIM))
    log_gamma = jnp.log(1.0 - 2.0 ** (-5.0 - jnp.arange(N_HEADS, dtype=F32)))
    idx = jnp.arange(chunk, dtype=F32)
    rel = idx[:, None] - idx[None, :]
    decay = jnp.where(rel >= 0, jnp.exp(log_gamma[:, None, None] * jnp.maximum(rel, 0.0)), 0.0)
    k_w = jnp.exp(log_gamma[:, None] * (chunk - 1.0 - idx)[None, :])
    q_w = jnp.exp(log_gamma[:, None] * (idx + 1.0)[None, :])
    per_lane = lambda t: jnp.repeat(t.T, RET_K_DIM, axis=1)
    chunk_decay = jnp.exp(log_gamma * chunk)
    cd = jnp.broadcast_to(jnp.repeat(chunk_decay, RET_K_DIM).reshape(N_HEADS // 2, LANES, 1),
                          (N_HEADS // 2, LANES, 2 * HEAD_DIM))
    r = jnp.arange(LANES)[:, None] // RET_K_DIM
    c = jnp.arange(2 * HEAD_DIM)[None, :] // HEAD_DIM
    bd = (r == c).astype(F32)
    k_scale = RET_K_DIM ** -0.5
    return cos, sin, decay * k_scale, per_lane(q_w), per_lane(k_w) * k_scale, cd, bd


def _retention(z_hi):
    bsz, seq, _ = z_hi.shape
    chunk = min(256, seq)
    cos, sin, decay, q_w, k_w, cd, bd = _retention_tables(seq, chunk)
    d_qk = N_HEADS * RET_K_DIM
    const = lambda shape: pl.BlockSpec(shape, lambda b, n: (0,) * len(shape))
    return pl.pallas_call(
        functools.partial(_ret_kernel, chunk=chunk),
        grid=(bsz, seq // chunk),
        in_specs=[pl.BlockSpec((1, chunk, d_qk), lambda b, n: (b, n, ZHI_RQ // d_qk)),
                  pl.BlockSpec((1, chunk, d_qk), lambda b, n: (b, n, ZHI_RK // d_qk)),
                  pl.BlockSpec((1, chunk, D_GROUP), lambda b, n: (b, n, ZHI_RV // D_GROUP)),
                  pl.BlockSpec((1, chunk, D_GROUP), lambda b, n: (b, n, ZHI_RG // D_GROUP)),
                  pl.BlockSpec((chunk, LANES), lambda b, n: (n, 0)),
                  pl.BlockSpec((chunk, LANES), lambda b, n: (n, 0)),
                  const((N_HEADS, chunk, chunk)), const((chunk, d_qk)), const((chunk, d_qk)),
                  const((N_HEADS // 2, LANES, 2 * HEAD_DIM)), const((LANES, 2 * HEAD_DIM))],
        out_specs=pl.BlockSpec((1, chunk, D_GROUP), lambda b, n: (b, n, 0)),
        out_shape=jax.ShapeDtypeStruct((bsz, seq, D_GROUP), BF16),
        scratch_shapes=[pltpu.VMEM((N_HEADS // 2, LANES, 2 * HEAD_DIM), F32)],
        compiler_params=_cparams(("arbitrary", "arbitrary"), 32),
        name="retention",
    )(z_hi, z_hi, z_hi, z_hi, cos, sin, decay, q_w, k_w, cd, bd)


def _gelu_tanh(x):
    return 0.5 * x * (1.0 + jnp.tanh(math.sqrt(2.0 / math.pi) * (x + 0.044715 * (x * x * x))))


def _sg_kernel(u_ref, v_ref, w_ref, bt_ref, lg_ref, lb_ref, o_ref, *, bs):
    u = _gelu_tanh(u_ref[0].astype(F32))
    v = _layer_norm_rows(_gelu_tanh(v_ref[0].astype(F32)), lg_ref[...], lb_ref[...]).astype(BF16)
    row = lax.broadcasted_iota(jnp.int32, (SG_CHUNK, SG_CHUNK), 0)
    col = lax.broadcasted_iota(jnp.int32, (SG_CHUNK, SG_CHUNK), 1)
    causal = col <= row
    for g in range(N_HEADS):
        cs = slice(g * HEAD_DIM, (g + 1) * HEAD_DIM)
        w_g = jnp.where(causal, w_ref[g], 0.0).astype(BF16)
        bias = bt_ref[:, g:g + 1]
        for r in range(bs // SG_CHUNK):
            rs = slice(r * SG_CHUNK, (r + 1) * SG_CHUNK)
            mixed = jnp.dot(w_g, v[rs, cs], preferred_element_type=F32) + bias
            o_ref[0, rs, cs] = (u[rs, cs] * mixed).astype(BF16)


def _spatial_gating(z_hi, sg_w, sg_b, ln_g, ln_b):
    bsz, seq, _ = z_hi.shape
    bs = min(512, seq)
    vec = lambda t: t.reshape(1, D_GROUP)
    tile = lambda off: pl.BlockSpec((1, bs, D_GROUP), lambda b, i: (b, i, off // D_GROUP))
    return pl.pallas_call(
        functools.partial(_sg_kernel, bs=bs),
        grid=(bsz, seq // bs),
        in_specs=[tile(ZHI_SGU), tile(ZHI_SGV),
                  pl.BlockSpec((N_HEADS, SG_CHUNK, SG_CHUNK), lambda b, i: (0, 0, 0)),
                  pl.BlockSpec((SG_CHUNK, N_HEADS), lambda b, i: (0, 0)),
                  pl.BlockSpec((1, D_GROUP), lambda b, i: (0, 0)),
                  pl.BlockSpec((1, D_GROUP), lambda b, i: (0, 0))],
        out_specs=pl.BlockSpec((1, bs, D_GROUP), lambda b, i: (b, i, 0)),
        out_shape=jax.ShapeDtypeStruct((bsz, seq, D_GROUP), BF16),
        compiler_params=_cparams(("arbitrary", "arbitrary"), 32),
        name="spatial_gating",
    )(z_hi, z_hi, sg_w, sg_b.T, vec(ln_g), vec(ln_b))


def _forget_weights(w_gate_l, f_bias_l):
    d = w_gate_l.shape[0]
    wf = jnp.concatenate([w_gate_l, w_gate_l, w_gate_l, jnp.zeros((d, LANES - 3 * N_HEADS), F32)], axis=1).astype(BF16)
    fb = jnp.concatenate([f_bias_l, f_bias_l, f_bias_l, jnp.zeros((LANES - 3 * N_HEADS,), F32)]).reshape(1, LANES)
    return wf, fb


def kernel(x, c, ada_w, ada_b, mix_pre_g, mix_post_g, w_in, fox_f_bias, conv_w, conv_b, conv_ln_g, conv_ln_b,
           sg_w, sg_b, sg_ln_g, sg_ln_b, w_out, ffn_pre_g, ffn_post_g, w_ff1, w_ff2):
    bsz, seq, d = x.shape
    n_layers = ada_w.shape[0]
    m = bsz * seq
    mod = _modulation(c, ada_w, ada_b).reshape(n_layers, bsz, 6, 1, d)
    shift1, scale1, gate1, shift2, scale2, gate2 = [mod[:, :, t] for t in range(6)]

    q_scale = jnp.ones((w_in.shape[2],), F32).at[ZLO_FQ:ZLO_FQ + D_GROUP].set(FOX_SCALE)
    w_in_b = (w_in * q_scale).astype(BF16)
    w_lo_b = w_in_b[:, :, :FF_OFF]
    w_hi_b = w_in_b[:, :, FF_OFF + N_HEADS:]
    w_gate = w_in[:, :, FF_OFF:FF_OFF + N_HEADS]

    x2 = x.reshape(m, d)
    h, c3 = _rows(x2, seq, pre=(mix_pre_g[0], scale1[0], shift1[0]),
                  forget=_forget_weights(w_gate[0], fox_f_bias[0]))
    for l in range(n_layers):
        z_lo, w_out_b = _matmul([h], w_lo_b, l, BF16, side=(w_out, l))
        z_hi, w_ff1_b = _matmul([h], w_hi_b, l, BF16, side=(w_ff1, l))
        z_lo = z_lo.reshape(bsz, seq, D_ZHALF)
        z_hi = z_hi.reshape(bsz, seq, D_ZHALF)
        y_conv = _conv_module(z_lo, conv_w[l], conv_b[l], conv_ln_g[l], conv_ln_b[l])
        y_fox = _forgetting_attention(z_lo, c3)
        y_ret = _retention(z_hi)
        y_sg = _spatial_gating(z_hi, sg_w[l], sg_b[l], sg_ln_g[l], sg_ln_b[l])
        parts = [t.reshape(m, D_GROUP) for t in (y_conv, y_fox, y_ret, y_sg)]
        y = _matmul(parts, w_out_b, 0, F32)
        x2, h = _rows(x2, seq, post=(y, mix_post_g[l], gate1[l]), pre=(ffn_pre_g[l], scale2[l], shift2[l]))
        u, w_ff2_b = _matmul([h], w_ff1_b, 0, BF16, relu2=True, side=(w_ff2, l))
        y = _matmul_ksplit(u, w_ff2_b, 0)
        if l + 1 < n_layers:
            x2, h, c3 = _rows(x2, seq, post=(y, ffn_post_g[l], gate2[l]),
                              pre=(mix_pre_g[l + 1], scale1[l + 1], shift1[l + 1]),
                              forget=_forget_weights(w_gate[l + 1], fox_f_bias[l + 1]))
        else:
            (x2,) = _rows(x2, seq, post=(y, ffn_post_g[l], gate2[l]))
    return x2.reshape(bsz, seq, d)
```

```python
import functools
import math

import jax
import jax.numpy as jnp
from jax import lax
from jax.experimental import pallas as pl
from jax.experimental.pallas import tpu as pltpu

F32 = jnp.float32
BF16 = jnp.bfloat16

D_GROUP = 1024
N_HEADS = 8
HEAD_DIM = D_GROUP // N_HEADS
CONV_WIDTH = 31
CONV_HALO = 32
RET_K_DIM = HEAD_DIM // 2
SG_CHUNK = 128
ROPE_BASE = 10000.0
EPS = 1e-6
GN_EPS = 1e-5
LOG2E = math.log2(math.e)
FOX_SCALE = LOG2E * HEAD_DIM ** -0.5

D_ZHALF = 5120
FF_OFF = 5120
ZLO_GLU_A, ZLO_GLU_B, ZLO_FQ, ZLO_FK, ZLO_FV = 0, 1024, 2048, 3072, 4096
ZHI_RQ, ZHI_RK, ZHI_RV, ZHI_RG, ZHI_SGU, ZHI_SGV = 0, 512, 1024, 2048, 3072, 4096

MIB = 1024 * 1024
LANES = 128
SUBLANES = 8


def _cparams(semantics, vmem_mib):
    return pltpu.CompilerParams(dimension_semantics=semantics, vmem_limit_bytes=vmem_mib * MIB)


def _split3(x):
    p1 = x.astype(BF16)
    r1 = x - p1.astype(F32)
    p2 = r1.astype(BF16)
    p3 = (r1 - p2.astype(F32)).astype(BF16)
    return p1, p2, p3


def _mod_kernel(ct_ref, w_ref, b_ref, o_ref, sb_ref, acc_ref, *, bsz):
    q = pl.program_id(1)

    @pl.when((pl.program_id(0) == 0) & (q == 0))
    def _():
        for b in range(bsz):
            cb = ct_ref[:, b:b + 1]
            sb_ref[b] = jnp.broadcast_to(cb * jax.nn.sigmoid(cb), sb_ref.shape[1:])

    @pl.when(q == 0)
    def _():
        acc_ref[...] = jnp.zeros_like(acc_ref)

    kc, n = w_ref.shape[1], w_ref.shape[2]
    start = pl.multiple_of(q * kc, kc)
    s_rows = [sb_ref[b, pl.ds(start, kc), :] for b in range(bsz)]
    for cg in range(n // LANES):
        cs = slice(cg * LANES, (cg + 1) * LANES)
        w = w_ref[0, :, cs]
        for b in range(bsz):
            acc_ref[b, :, cs] += jnp.sum((w * s_rows[b]).reshape(kc // SUBLANES, SUBLANES, LANES), axis=0)

    @pl.when(q == pl.num_programs(1) - 1)
    def _():
        o_ref[0] = jnp.zeros(o_ref.shape[1:], F32)
        for b in range(bsz):
            o_ref[0, b:b + 1, :] = jnp.sum(acc_ref[b], axis=0, keepdims=True) + b_ref[0]


def _modulation(c, ada_w, ada_b):
    n_layers, d, n = ada_w.shape
    bsz = c.shape[0]
    rows = SUBLANES
    kc = 128
    out = pl.pallas_call(
        functools.partial(_mod_kernel, bsz=bsz),
        grid=(n_layers, d // kc),
        in_specs=[
            pl.BlockSpec((d, bsz), lambda l, q: (0, 0)),
            pl.BlockSpec((1, kc, n), lambda l, q: (l, q, 0)),
            pl.BlockSpec((1, 1, n), lambda l, q: (l, 0, 0)),
        ],
        out_specs=pl.BlockSpec((1, rows, n), lambda l, q: (l, 0, 0)),
        out_shape=jax.ShapeDtypeStruct((n_layers, rows, n), F32),
        scratch_shapes=[pltpu.VMEM((bsz, d, LANES), F32), pltpu.VMEM((bsz, SUBLANES, n), F32)],
        compiler_params=_cparams(("arbitrary", "arbitrary"), 48),
        name="adaln_mod",
    )(c.T, ada_w, ada_b.reshape(n_layers, 1, n))
    return out[:, :bsz]


def _log_sigmoid(x):
    return jnp.minimum(x, 0.0) - jnp.log(1.0 + jnp.exp(-jnp.abs(x)))


def _rows_kernel(*refs, has_post, has_pre, has_forget, tiles_per_seq):
    it = iter(refs)
    x_ref = next(it)
    if has_post:
        y_ref, gpost_ref, gate_ref = next(it), next(it), next(it)
    if has_pre:
        gpre_ref, sc_ref, sh_ref = next(it), next(it), next(it)
    if has_forget:
        wf_ref, fb_ref, tri_ref = next(it), next(it), next(it)
    if has_post:
        xo_ref = next(it)
    if has_pre:
        h_ref = next(it)
    if has_forget:
        c3_ref, carry_ref = next(it), next(it)

    x = x_ref[...]
    if has_post:
        y = y_ref[...].astype(F32)
        r = lax.rsqrt(jnp.mean(y * y, axis=-1, keepdims=True) + EPS)
        x = x + gate_ref[0] * (y * r * gpost_ref[...])
        xo_ref[...] = x
    if not has_pre:
        return
    r = lax.rsqrt(jnp.mean(x * x, axis=-1, keepdims=True) + EPS)
    hb = (x * r * (gpre_ref[...] * (1.0 + sc_ref[0])) + sh_ref[0]).astype(BF16)
    h_ref[...] = hb
    if not has_forget:
        return
    logf = _log_sigmoid(jnp.dot(hb, wf_ref[...], preferred_element_type=F32) + fb_ref[...])
    tri = tri_ref[...]
    local = sum(jnp.dot(tri, p, preferred_element_type=F32) for p in _split3(logf))

    @pl.when(pl.program_id(0) % tiles_per_seq == 0)
    def _():
        carry_ref[...] = jnp.zeros_like(carry_ref)

    cum = carry_ref[...] + local
    bm = cum.shape[0]
    carry_ref[...] = cum[bm - 1:bm, :]
    n1, n2, n3 = _split3(cum * (-LOG2E))
    lane = lax.broadcasted_iota(jnp.int32, cum.shape, 1)
    zero = jnp.zeros_like(n1)
    c3_ref[0] = jnp.where(lane < N_HEADS, n1,
                          jnp.where(lane < 2 * N_HEADS, n2, jnp.where(lane < 3 * N_HEADS, n3, zero)))


def _rows(x2, seq, post=None, pre=None, forget=None):
    m, d = x2.shape
    bsz = m // seq
    bm = min(256, seq)
    per_seq = seq // bm
    row = pl.BlockSpec((bm, d), lambda i: (i, 0))
    vec = pl.BlockSpec((1, d), lambda i: (0, 0))
    per_batch = pl.BlockSpec((1, 1, d), lambda i: (i // per_seq, 0, 0))
    args, in_specs, out_specs, out_shape, scratch = [x2], [row], [], [], []
    if post is not None:
        y, g_post, gate = post
        args += [y, g_post.reshape(1, d), gate]
        in_specs += [row, vec, per_batch]
    if pre is not None:
        g_pre, scale, shift = pre
        args += [g_pre.reshape(1, d), scale, shift]
        in_specs += [vec, per_batch, per_batch]
    if forget is not None:
        wf, fb = forget
        tri = (lax.broadcasted_iota(jnp.int32, (bm, bm), 1) <= lax.broadcasted_iota(jnp.int32, (bm, bm), 0)).astype(BF16)
        args += [wf, fb, tri]
        in_specs += [pl.BlockSpec((d, LANES), lambda i: (0, 0)), pl.BlockSpec((1, LANES), lambda i: (0, 0)),
                     pl.BlockSpec((bm, bm), lambda i: (0, 0))]
    if post is not None:
        out_specs.append(row)
        out_shape.append(jax.ShapeDtypeStruct((m, d), F32))
    if pre is not None:
        out_specs.append(row)
        out_shape.append(jax.ShapeDtypeStruct((m, d), BF16))
    if forget is not None:
        out_specs.append(pl.BlockSpec((1, bm, LANES), lambda i: (i // per_seq, i % per_seq, 0)))
        out_shape.append(jax.ShapeDtypeStruct((bsz, seq, LANES), BF16))
        scratch.append(pltpu.VMEM((1, LANES), F32))
    return pl.pallas_call(
        functools.partial(_rows_kernel, has_post=post is not None, has_pre=pre is not None,
                          has_forget=forget is not None, tiles_per_seq=per_seq),
        grid=(m // bm,),
        in_specs=in_specs, out_specs=out_specs, out_shape=out_shape, scratch_shapes=scratch,
        compiler_params=_cparams(("arbitrary",), 48),
        name="rows_norm",
    )(*args)


def _mm_kernel(*refs, n_a, relu2, has_side, w_t):
    a_refs, w_ref = refs[:n_a], refs[n_a]
    if has_side:
        side_in_ref, o_ref, side_out_ref = refs[n_a + 1:n_a + 4]
        side_out_ref[...] = side_in_ref[...].astype(BF16)
    else:
        o_ref = refs[n_a + 1]
    if w_t:
        assert n_a == 1
        acc = lax.dot_general(a_refs[0][...], w_ref[0], (((1,), (1,)), ((), ())), preferred_element_type=F32)
    else:
        ksz = w_ref.shape[1] // n_a
        acc = None
        for t, a_ref in enumerate(a_refs):
            part = jnp.dot(a_ref[...], w_ref[0, t * ksz:(t + 1) * ksz, :], preferred_element_type=F32)
            acc = part if acc is None else acc + part
    if relu2:
        acc = jnp.square(jnp.maximum(acc, 0.0))
    o_ref[...] = acc.astype(o_ref.dtype)


def _matmul(a_parts, w_all, layer, out_dtype, relu2=False, side=None, w_t=False):
    m = a_parts[0].shape[0]
    if w_t:
        _, n, k = w_all.shape
    else:
        _, k, n = w_all.shape
    ka = k // len(a_parts)
    bm = min(1024, m)
    bn = min(1024, n)
    nn = n // bn
    in_specs = [pl.BlockSpec((bm, ka), lambda i, j: (i, 0)) for _ in a_parts]
    if w_t:
        in_specs.append(pl.BlockSpec((1, bn, k), lambda i, j: (layer, j, 0)))
    else:
        in_specs.append(pl.BlockSpec((1, k, bn), lambda i, j: (layer, 0, j)))
    out_specs = pl.BlockSpec((bm, bn), lambda i, j: (i, j))
    out_shape = jax.ShapeDtypeStruct((m, n), out_dtype)
    args = list(a_parts) + [w_all]
    if side is not None:
        src, src_layer = side
        _, rows, cols = src.shape
        n_slabs = 1
        while n_slabs * 2 <= (m // bm) * nn and rows % (n_slabs * 2) == 0 and rows // (n_slabs * 2) >= 2 * SUBLANES:
            n_slabs *= 2
        slab = rows // n_slabs
        slab_idx = lambda i, j: jnp.minimum(i * nn + j, n_slabs - 1)
        in_specs.append(pl.BlockSpec((1, slab, cols), lambda i, j: (src_layer, slab_idx(i, j), 0)))
        out_specs = [out_specs, pl.BlockSpec((1, slab, cols), lambda i, j: (0, slab_idx(i, j), 0))]
        out_shape = [out_shape, jax.ShapeDtypeStruct((1, rows, cols), BF16)]
        args.append(src)
    return pl.pallas_call(
        functools.partial(_mm_kernel, n_a=len(a_parts), relu2=relu2, has_side=side is not None, w_t=w_t),
        grid=(m // bm, nn),
        in_specs=in_specs,
        out_specs=out_specs,
        out_shape=out_shape,
        compiler_params=_cparams(("arbitrary", "arbitrary"), 56),
        name="matmul_fullk",
    )(*args)


def _mm_acc_kernel(a_ref, w_ref, o_ref, acc_ref):
    q = pl.program_id(2)
    y = jnp.dot(a_ref[...], w_ref[0], preferred_element_type=F32)

    @pl.when(q == 0)
    def _():
        acc_ref[...] = y

    @pl.when((q != 0) & (q != pl.num_programs(2) - 1))
    def _():
        acc_ref[...] += y

    @pl.when(q == pl.num_programs(2) - 1)
    def _():
        o_ref[...] = (acc_ref[...] + y).astype(o_ref.dtype)


def _matmul_ksplit(a, w_all, layer, out_dtype):
    m, k = a.shape
    n = w_all.shape[2]
    bm = min(1024, m)
    bn = min(1024, n)
    bk = min(4096, k // 2)
    return pl.pallas_call(
        _mm_acc_kernel,
        grid=(m // bm, n // bn, k // bk),
        in_specs=[
            pl.BlockSpec((bm, bk), lambda i, j, q: (i, q)),
            pl.BlockSpec((1, bk, bn), lambda i, j, q: (layer, q, j)),
        ],
        out_specs=pl.BlockSpec((bm, bn), lambda i, j, q: (i, j)),
        out_shape=jax.ShapeDtypeStruct((m, n), out_dtype),
        scratch_shapes=[pltpu.VMEM((bm, bn), F32)],
        compiler_params=_cparams(("arbitrary", "arbitrary", "arbitrary"), 52),
        name="matmul_ksplit",
    )(a, w_all)


def _layer_norm_rows(h, g, b):
    mu = jnp.mean(h, axis=-1, keepdims=True)
    hc = h - mu
    var = jnp.mean(hc * hc, axis=-1, keepdims=True)
    return hc * lax.rsqrt(var + EPS) * g + b


def _conv_kernel(a_ref, b_ref, ah_ref, bh_ref, w_ref, cb_ref, lg_ref, lb_ref, o_ref, buf_ref, acc_ref, *, bs):
    i = pl.program_id(1)
    glu = a_ref[0].astype(F32) * jax.nn.sigmoid(b_ref[0].astype(F32))
    halo = ah_ref[0].astype(F32) * jax.nn.sigmoid(bh_ref[0].astype(F32))
    halo = jnp.where(i == 0, 0.0, halo)
    n_cb = D_GROUP // LANES
    for cb in range(n_cb):
        cs = slice(cb * LANES, (cb + 1) * LANES)
        buf_ref[cb, 0:CONV_HALO, :] = halo[:, cs]
        buf_ref[cb, CONV_HALO:CONV_HALO + bs, :] = glu[:, cs]
    first = CONV_HALO - (CONV_WIDTH - 1)
    n_u = bs // SUBLANES

    def per_col_block(cb, carry):
        taps = [jnp.broadcast_to(w_ref[cb, j:j + 1, :], (SUBLANES, LANES)) for j in range(CONV_WIDTH)]
        acc = [jnp.broadcast_to(cb_ref[cb], (SUBLANES, LANES)) for _ in range(n_u)]
        for r in range(SUBLANES):
            js = [j for j in range(CONV_WIDTH) if (first + j) % SUBLANES == r]
            ks = [(first + j) // SUBLANES for j in js]
            for v in range(min(ks), n_u + max(ks)):
                window = buf_ref[cb, r + SUBLANES * v:r + SUBLANES * (v + 1), :]
                for j, k in zip(js, ks):
                    if 0 <= v - k < n_u:
                        acc[v - k] = acc[v - k] + taps[j] * window
        for u in range(n_u):
            acc_ref[cb, u * SUBLANES:(u + 1) * SUBLANES, :] = acc[u]
        return carry

    lax.fori_loop(0, n_cb, per_col_block, 0)
    h = jnp.concatenate([acc_ref[cb] for cb in range(n_cb)], axis=1)
    h = _layer_norm_rows(h, lg_ref[...], lb_ref[...])
    o_ref[0] = (h * jax.nn.sigmoid(h)).astype(BF16)


def _conv_module(z_lo, conv_w, conv_b, ln_g, ln_b):
    bsz, seq, _ = z_lo.shape
    bs = min(128, seq)
    hb = bs // CONV_HALO
    vec = lambda v: v.reshape(1, D_GROUP)
    tile = lambda col: pl.BlockSpec((1, bs, D_GROUP), lambda b, i: (b, i, col))
    halo = lambda col: pl.BlockSpec((1, CONV_HALO, D_GROUP), lambda b, i: (b, jnp.maximum(i * hb - 1, 0), col))
    full = lambda r: pl.BlockSpec((r, D_GROUP), lambda b, i: (0, 0))
    n_cb = D_GROUP // LANES
    per_cb = lambda rows: pl.BlockSpec((n_cb, rows, LANES), lambda b, i: (0, 0, 0))
    w_cb = conv_w.reshape(CONV_WIDTH, n_cb, LANES).transpose(1, 0, 2)
    return pl.pallas_call(
        functools.partial(_conv_kernel, bs=bs),
        grid=(bsz, seq // bs),
        in_specs=[tile(ZLO_GLU_A // D_GROUP), tile(ZLO_GLU_B // D_GROUP),
                  halo(ZLO_GLU_A // D_GROUP), halo(ZLO_GLU_B // D_GROUP),
                  per_cb(CONV_WIDTH), per_cb(1), full(1), full(1)],
        out_specs=pl.BlockSpec((1, bs, D_GROUP), lambda b, i: (b, i, 0)),
        out_shape=jax.ShapeDtypeStruct((bsz, seq, D_GROUP), BF16),
        scratch_shapes=[pltpu.VMEM((n_cb, CONV_HALO + bs, LANES), F32),
                        pltpu.VMEM((n_cb, bs, LANES), F32)],
        compiler_params=_cparams(("arbitrary", "arbitrary"), 32),
        name="conv_module",
    )(z_lo, z_lo, z_lo, z_lo, w_cb, conv_b.reshape(n_cb, 1, LANES), vec(ln_g), vec(ln_b))


def _fox_kernel(q_ref, k_ref, v_ref, c3_ref, o_ref, ka_ref, va_ref, s0_ref, s1_ref, p_ref, m_ref, acc_ref,
                *, blk, sub, rg):
    h = pl.program_id(1)
    i = pl.program_id(2)

    @pl.when(i == 0)
    def _():
        ka_ref[:, :HEAD_DIM] = k_ref[0]
        ka_ref[:, HEAD_DIM:] = c3_ref[0]
        va_ref[:, :HEAD_DIM] = v_ref[0]
        lane = lax.broadcasted_iota(jnp.int32, (va_ref.shape[0], HEAD_DIM), 1)
        va_ref[:, HEAD_DIM:] = (lane == 0).astype(BF16)

    lane = lax.broadcasted_iota(jnp.int32, (blk, HEAD_DIM), 1)
    pick = (lane == h) | (lane == h + N_HEADS) | (lane == h + 2 * N_HEADS)
    qa = jnp.concatenate([q_ref[0], pick.astype(BF16)], axis=1)
    m_ref[...] = jnp.full(m_ref.shape, -jnp.inf, F32)
    acc_ref[...] = jnp.zeros(acc_ref.shape, F32)

    def logits_to(s_ref, jsub, r0=0):
        start = pl.multiple_of(jsub * sub, sub)
        s_ref[r0:, :] = lax.dot_general(qa[r0:], ka_ref[pl.ds(start, sub), :], (((1,), (1,)), ((), ())),
                                        preferred_element_type=F32)

    def update(s_ref, jsub, mask_off=None, r0=0):
        for r in range(r0 // rg, blk // rg):
            rows = slice(r * rg, (r + 1) * rg)
            s = s_ref[rows, :]
            if mask_off is not None and r * rg < mask_off + sub - 1:
                row = lax.broadcasted_iota(jnp.int32, (rg, sub), 0) + r * rg
                col = lax.broadcasted_iota(jnp.int32, (rg, sub), 1) + mask_off
                s = jnp.where(col <= row, s, -jnp.inf)
            m_old = m_ref[rows, :]
            m_new = jnp.maximum(m_old, jnp.max(s, axis=-1, keepdims=True))
            m_ref[rows, :] = m_new
            alpha = jnp.exp2(m_old - m_new)
            for cg in range(sub // LANES):
                cs = slice(cg * LANES, (cg + 1) * LANES)
                p_ref[rows, cs] = jnp.exp2(s[:, cs] - m_new).astype(BF16)
            for cg in range(acc_ref.shape[1] // LANES):
                cs = slice(cg * LANES, (cg + 1) * LANES)
                acc_ref[rows, cs] = alpha * acc_ref[rows, cs]
        start = pl.multiple_of(jsub * sub, sub)
        acc_ref[r0:, :] += jnp.dot(p_ref[r0:, :], va_ref[pl.ds(start, sub), :], preferred_element_type=F32)

    per = blk // sub
    logits_to(s0_ref, 0)

    def body(t, carry):
        logits_to(s1_ref, per * t + 1)
        update(s0_ref, per * t)
        logits_to(s0_ref, per * t + 2)
        update(s1_ref, per * t + 1)
        return carry

    lax.fori_loop(0, i, body, 0)
    logits_to(s1_ref, per * i + 1, r0=sub)
    update(s0_ref, per * i, mask_off=0)
    update(s1_ref, per * i + 1, mask_off=sub, r0=sub)
    acc = acc_ref[...]
    o_ref[0] = (acc[:, :HEAD_DIM] * (1.0 / acc[:, HEAD_DIM:HEAD_DIM + 1])).astype(BF16)


def _forgetting_attention(z_lo, c3):
    bsz, seq, _ = z_lo.shape
    blk = min(1024, seq)
    sub = blk // 2
    rg = min(128, blk)
    col = lambda off: off // HEAD_DIM
    kv = lambda off: pl.BlockSpec((1, seq, HEAD_DIM), lambda b, h, i: (b, 0, col(off) + h))
    return pl.pallas_call(
        functools.partial(_fox_kernel, blk=blk, sub=sub, rg=rg),
        grid=(bsz, N_HEADS, seq // blk),
        in_specs=[pl.BlockSpec((1, blk, HEAD_DIM), lambda b, h, i: (b, i, col(ZLO_FQ) + h)),
                  kv(ZLO_FK), kv(ZLO_FV),
                  pl.BlockSpec((1, seq, LANES), lambda b, h, i: (b, 0, 0))],
        out_specs=pl.BlockSpec((1, blk, HEAD_DIM), lambda b, h, i: (b, i, h)),
        out_shape=jax.ShapeDtypeStruct((bsz, seq, D_GROUP), BF16),
        scratch_shapes=[pltpu.VMEM((seq, 2 * HEAD_DIM), BF16), pltpu.VMEM((seq, 2 * HEAD_DIM), BF16),
                        pltpu.VMEM((blk, sub), F32), pltpu.VMEM((blk, sub), F32), pltpu.VMEM((blk, sub), BF16),
                        pltpu.VMEM((blk, LANES), F32),
                        pltpu.VMEM((blk, 2 * HEAD_DIM), F32)],
        compiler_params=_cparams(("arbitrary", "arbitrary", "arbitrary"), 48),
        name="forgetting_attention",
    )(z_lo, z_lo, z_lo, c3)


def _ret_kernel(q_ref, k_ref, v_ref, g_ref, cos_ref, sin_ref, dec_ref, qw_ref, kw_ref, cd_ref, bd_ref,
                o_ref, state_ref, *, chunk):
    n = pl.program_id(1)

    @pl.when(n == 0)
    def _():
        state_ref[...] = jnp.zeros_like(state_ref)

    cos = cos_ref[...]
    sin = sin_ref[...]
    lane = lax.broadcasted_iota(jnp.int32, (chunk, LANES), 1)
    first_half = (lane & (RET_K_DIM // 2)) == 0
    low_head = lane < RET_K_DIM

    def rotate(x):
        swapped = jnp.where(first_half, pltpu.roll(x, LANES - RET_K_DIM // 2, 1), pltpu.roll(x, RET_K_DIM // 2, 1))
        return x * cos + swapped * sin

    for pr in range(N_HEADS // 2):
        ls = slice(pr * LANES, (pr + 1) * LANES)
        qr = rotate(q_ref[0, :, ls].astype(F32))
        kr = rotate(k_ref[0, :, ls].astype(F32))
        kb = kr.astype(BF16)
        v_pair = v_ref[0, :, pr * 2 * HEAD_DIM:(pr + 1) * 2 * HEAD_DIM]
        state = state_ref[pr]
        cross = jnp.dot((qr * qw_ref[:, ls]).astype(BF16), state.astype(BF16), preferred_element_type=F32)
        for a in range(2):
            head = 2 * pr + a
            hs = slice(head * HEAD_DIM, (head + 1) * HEAD_DIM)
            qa = jnp.where(low_head if a == 0 else jnp.logical_not(low_head), qr, 0.0).astype(BF16)
            scores = lax.dot_general(qa, kb, (((1,), (1,)), ((), ())), preferred_element_type=F32) * dec_ref[head]
            inner = jnp.dot(scores.astype(BF16), v_pair[:, a * HEAD_DIM:(a + 1) * HEAD_DIM],
                            preferred_element_type=F32)
            y = inner + cross[:, a * HEAD_DIM:(a + 1) * HEAD_DIM]
            mu = jnp.mean(y, axis=-1, keepdims=True)
            yc = y - mu
            var = jnp.mean(yc * yc, axis=-1, keepdims=True)
            yn = yc * lax.rsqrt(var + GN_EPS)
            gate = g_ref[0, :, hs].astype(F32)
            o_ref[0, :, hs] = (gate * jax.nn.sigmoid(gate) * yn).astype(BF16)
        kw_t = (kr * kw_ref[:, ls]).T.astype(BF16)
        kv = jnp.dot(kw_t, v_pair, preferred_element_type=F32)
        state_ref[pr] = state * cd_ref[pr] + kv * bd_ref[...]


def _retention_tables(seq, chunk):
    half = RET_K_DIM // 2
    inv = 1.0 / (ROPE_BASE ** jnp.linspace(0.0, 1.0, half, dtype=F32))
    ang = jnp.arange(seq).astype(F32)[:, None] * inv[None, :]
    cos = jnp.tile(jnp.cos(ang), (1, LANES // half))
    sin = jnp.tile(jnp.concatenate([-jnp.sin(ang), jnp.sin(ang)], axis=-1), (1, LANES // RET_K_DIM))
    log_gamma = jnp.log(1.0 - 2.0 ** (-5.0 - jnp.arange(N_HEADS, dtype=F32)))
    idx = jnp.arange(chunk, dtype=F32)
    rel = idx[:, None] - idx[None, :]
    decay = jnp.where(rel >= 0, jnp.exp(log_gamma[:, None, None] * jnp.maximum(rel, 0.0)), 0.0)
    k_w = jnp.exp(log_gamma[:, None] * (chunk - 1.0 - idx)[None, :])
    q_w = jnp.exp(log_gamma[:, None] * (idx + 1.0)[None, :])
    per_lane = lambda t: jnp.repeat(t.T, RET_K_DIM, axis=1)
    chunk_decay = jnp.exp(log_gamma * chunk)
    cd = jnp.broadcast_to(jnp.repeat(chunk_decay, RET_K_DIM).reshape(N_HEADS // 2, LANES, 1),
                          (N_HEADS // 2, LANES, 2 * HEAD_DIM))
    r = jnp.arange(LANES)[:, None] // RET_K_DIM
    c = jnp.arange(2 * HEAD_DIM)[None, :] // HEAD_DIM
    bd = (r == c).astype(F32)
    k_scale = RET_K_DIM ** -0.5
    return cos, sin, decay * k_scale, per_lane(q_w), per_lane(k_w) * k_scale, cd, bd


def _retention(z_hi):
    bsz, seq, _ = z_hi.shape
    chunk = min(256, seq)
    cos, sin, decay, q_w, k_w, cd, bd = _retention_tables(seq, chunk)
    d_qk = N_HEADS * RET_K_DIM
    const = lambda shape: pl.BlockSpec(shape, lambda b, n: (0,) * len(shape))
    return pl.pallas_call(
        functools.partial(_ret_kernel, chunk=chunk),
        grid=(bsz, seq // chunk),
        in_specs=[pl.BlockSpec((1, chunk, d_qk), lambda b, n: (b, n, ZHI_RQ // d_qk)),
                  pl.BlockSpec((1, chunk, d_qk), lambda b, n: (b, n, ZHI_RK // d_qk)),
                  pl.BlockSpec((1, chunk, D_GROUP), lambda b, n: (b, n, ZHI_RV // D_GROUP)),
                  pl.BlockSpec((1, chunk, D_GROUP), lambda b, n: (b, n, ZHI_RG // D_GROUP)),
                  pl.BlockSpec((chunk, LANES), lambda b, n: (n, 0)),
                  pl.BlockSpec((chunk, LANES), lambda b, n: (n, 0)),
                  const((N_HEADS, chunk, chunk)), const((chunk, d_qk)), const((chunk, d_qk)),
                  const((N_HEADS // 2, LANES, 2 * HEAD_DIM)), const((LANES, 2 * HEAD_DIM))],
        out_specs=pl.BlockSpec((1, chunk, D_GROUP), lambda b, n: (b, n, 0)),
        out_shape=jax.ShapeDtypeStruct((bsz, seq, D_GROUP), BF16),
        scratch_shapes=[pltpu.VMEM((N_HEADS // 2, LANES, 2 * HEAD_DIM), F32)],
        compiler_params=_cparams(("arbitrary", "arbitrary"), 32),
        name="retention",
    )(z_hi, z_hi, z_hi, z_hi, cos, sin, decay, q_w, k_w, cd, bd)


def _gelu_tanh(x):
    return 0.5 * x * (1.0 + jnp.tanh(math.sqrt(2.0 / math.pi) * (x + 0.044715 * (x * x * x))))


def _sg_kernel(u_ref, v_ref, w_ref, bt_ref, lg_ref, lb_ref, o_ref, *, bs):
    u = _gelu_tanh(u_ref[0].astype(F32))
    v = _layer_norm_rows(_gelu_tanh(v_ref[0].astype(F32)), lg_ref[...], lb_ref[...]).astype(BF16)
    row = lax.broadcasted_iota(jnp.int32, (SG_CHUNK, SG_CHUNK), 0)
    col = lax.broadcasted_iota(jnp.int32, (SG_CHUNK, SG_CHUNK), 1)
    causal = col <= row
    for g in range(N_HEADS):
        cs = slice(g * HEAD_DIM, (g + 1) * HEAD_DIM)
        w_g = jnp.where(causal, w_ref[g], 0.0).astype(BF16)
        bias = bt_ref[:, g:g + 1]
        for r in range(bs // SG_CHUNK):
            rs = slice(r * SG_CHUNK, (r + 1) * SG_CHUNK)
            mixed = jnp.dot(w_g, v[rs, cs], preferred_element_type=F32) + bias
            o_ref[0, rs, cs] = (u[rs, cs] * mixed).astype(BF16)


def _spatial_gating(z_hi, sg_w, sg_b, ln_g, ln_b):
    bsz, seq, _ = z_hi.shape
    bs = min(512, seq)
    vec = lambda t: t.reshape(1, D_GROUP)
    tile = lambda off: pl.BlockSpec((1, bs, D_GROUP), lambda b, i: (b, i, off // D_GROUP))
    return pl.pallas_call(
        functools.partial(_sg_kernel, bs=bs),
        grid=(bsz, seq // bs),
        in_specs=[tile(ZHI_SGU), tile(ZHI_SGV),
                  pl.BlockSpec((N_HEADS, SG_CHUNK, SG_CHUNK), lambda b, i: (0, 0, 0)),
                  pl.BlockSpec((SG_CHUNK, N_HEADS), lambda b, i: (0, 0)),
                  pl.BlockSpec((1, D_GROUP), lambda b, i: (0, 0)),
                  pl.BlockSpec((1, D_GROUP), lambda b, i: (0, 0))],
        out_specs=pl.BlockSpec((1, bs, D_GROUP), lambda b, i: (b, i, 0)),
        out_shape=jax.ShapeDtypeStruct((bsz, seq, D_GROUP), BF16),
        compiler_params=_cparams(("arbitrary", "arbitrary"), 32),
        name="spatial_gating",
    )(z_hi, z_hi, sg_w, sg_b.T, vec(ln_g), vec(ln_b))


def _forget_weights(w_gate_l, f_bias_l):
    d = w_gate_l.shape[0]
    wf = jnp.concatenate([w_gate_l, w_gate_l, w_gate_l, jnp.zeros((d, LANES - 3 * N_HEADS), F32)], axis=1).astype(BF16)
    fb = jnp.concatenate([f_bias_l, f_bias_l, f_bias_l, jnp.zeros((LANES - 3 * N_HEADS,), F32)]).reshape(1, LANES)
    return wf, fb


def kernel(x, c, ada_w, ada_b, mix_pre_g, mix_post_g, w_in, fox_f_bias, conv_w, conv_b, conv_ln_g, conv_ln_b,
           sg_w, sg_b, sg_ln_g, sg_ln_b, w_out, ffn_pre_g, ffn_post_g, w_ff1, w_ff2):
    bsz, seq, d = x.shape
    n_layers = ada_w.shape[0]
    m = bsz * seq
    mod = _modulation(c, ada_w, ada_b).reshape(n_layers, bsz, 6, 1, d)
    shift1, scale1, gate1, shift2, scale2, gate2 = [mod[:, :, t] for t in range(6)]

    w_in_t = jnp.transpose(w_in, (0, 2, 1))
    q_scale = jnp.ones((FF_OFF, 1), F32).at[ZLO_FQ:ZLO_FQ + D_GROUP].set(FOX_SCALE)
    w_lo_b = (w_in_t[:, :FF_OFF, :] * q_scale).astype(BF16)
    w_hi_b = w_in_t[:, FF_OFF + N_HEADS:, :].astype(BF16)
    w_gate = w_in[:, :, FF_OFF:FF_OFF + N_HEADS]

    x2 = x.reshape(m, d)
    h, c3 = _rows(x2, seq, pre=(mix_pre_g[0], scale1[0], shift1[0]),
                  forget=_forget_weights(w_gate[0], fox_f_bias[0]))
    for l in range(n_layers):
        z_lo, w_out_b = _matmul([h], w_lo_b, l, BF16, side=(w_out, l), w_t=True)
        z_hi, w_ff1_b = _matmul([h], w_hi_b, l, BF16, side=(w_ff1, l), w_t=True)
        z_lo = z_lo.reshape(bsz, seq, D_ZHALF)
        z_hi = z_hi.reshape(bsz, seq, D_ZHALF)
        y_conv = _conv_module(z_lo, conv_w[l], conv_b[l], conv_ln_g[l], conv_ln_b[l])
        y_fox = _forgetting_attention(z_lo, c3)
        y_ret = _retention(z_hi)
        y_sg = _spatial_gating(z_hi, sg_w[l], sg_b[l], sg_ln_g[l], sg_ln_b[l])
        parts = [t.reshape(m, D_GROUP) for t in (y_conv, y_fox, y_ret, y_sg)]
        y = _matmul(parts, w_out_b, 0, BF16)
        x2, h = _rows(x2, seq, post=(y, mix_post_g[l], gate1[l]), pre=(ffn_pre_g[l], scale2[l], shift2[l]))
        u, w_ff2_b = _matmul([h], w_ff1_b, 0, BF16, relu2=True, side=(w_ff2, l))
        y = _matmul_ksplit(u, w_ff2_b, 0, BF16)
        if l + 1 < n_layers:
            x2, h, c3 = _rows(x2, seq, post=(y, ffn_post_g[l], gate2[l]),
                              pre=(mix_pre_g[l + 1], scale1[l + 1], shift1[l + 1]),
                              forget=_forget_weights(w_gate[l + 1], fox_f_bias[l + 1]))
        else:
            (x2,) = _rows(x2, seq, post=(y, ffn_post_g[l], gate2[l]))
    return x2.reshape(bsz, seq, d)
```

```python
import functools
import math

import jax
import jax.numpy as jnp
from jax import lax
from jax.experimental import pallas as pl
from jax.experimental.pallas import tpu as pltpu

F32 = jnp.float32
BF16 = jnp.bfloat16

D_GROUP = 1024
N_HEADS = 8
HEAD_DIM = D_GROUP // N_HEADS
CONV_WIDTH = 31
CONV_HALO = 32
RET_K_DIM = HEAD_DIM // 2
SG_CHUNK = 128
ROPE_BASE = 10000.0
EPS = 1e-6
GN_EPS = 1e-5
LOG2E = math.log2(math.e)
FOX_SCALE = LOG2E * HEAD_DIM ** -0.5

D_ZHALF = 5120
FF_OFF = 5120
ZLO_GLU_A, ZLO_GLU_B, ZLO_FQ, ZLO_FK, ZLO_FV = 0, 1024, 2048, 3072, 4096
ZHI_RQ, ZHI_RK, ZHI_RV, ZHI_RG, ZHI_SGU, ZHI_SGV = 0, 512, 1024, 2048, 3072, 4096

MIB = 1024 * 1024
LANES = 128
SUBLANES = 8


def _cparams(semantics, vmem_mib):
    return pltpu.CompilerParams(dimension_semantics=semantics, vmem_limit_bytes=vmem_mib * MIB)


def _split3(x):
    p1 = x.astype(BF16)
    r1 = x - p1.astype(F32)
    p2 = r1.astype(BF16)
    p3 = (r1 - p2.astype(F32)).astype(BF16)
    return p1, p2, p3


def _mod_kernel(ct_ref, w_ref, b_ref, o_ref, sb_ref, acc_ref, *, bsz):
    q = pl.program_id(1)

    @pl.when((pl.program_id(0) == 0) & (q == 0))
    def _():
        for b in range(bsz):
            cb = ct_ref[:, b:b + 1]
            sb_ref[b] = jnp.broadcast_to(cb * jax.nn.sigmoid(cb), sb_ref.shape[1:])

    @pl.when(q == 0)
    def _():
        acc_ref[...] = jnp.zeros_like(acc_ref)

    kc, n = w_ref.shape[1], w_ref.shape[2]
    start = pl.multiple_of(q * kc, kc)
    s_rows = [sb_ref[b, pl.ds(start, kc), :] for b in range(bsz)]
    for cg in range(n // LANES):
        cs = slice(cg * LANES, (cg + 1) * LANES)
        w = w_ref[0, :, cs]
        for b in range(bsz):
            acc_ref[b, :, cs] += jnp.sum((w * s_rows[b]).reshape(kc // SUBLANES, SUBLANES, LANES), axis=0)

    @pl.when(q == pl.num_programs(1) - 1)
    def _():
        o_ref[0] = jnp.zeros(o_ref.shape[1:], F32)
        for b in range(bsz):
            o_ref[0, b:b + 1, :] = jnp.sum(acc_ref[b], axis=0, keepdims=True) + b_ref[0]


def _modulation(c, ada_w, ada_b):
    n_layers, d, n = ada_w.shape
    bsz = c.shape[0]
    rows = SUBLANES
    kc = 128
    out = pl.pallas_call(
        functools.partial(_mod_kernel, bsz=bsz),
        grid=(n_layers, d // kc),
        in_specs=[
            pl.BlockSpec((d, bsz), lambda l, q: (0, 0)),
            pl.BlockSpec((1, kc, n), lambda l, q: (l, q, 0)),
            pl.BlockSpec((1, 1, n), lambda l, q: (l, 0, 0)),
        ],
        out_specs=pl.BlockSpec((1, rows, n), lambda l, q: (l, 0, 0)),
        out_shape=jax.ShapeDtypeStruct((n_layers, rows, n), F32),
        scratch_shapes=[pltpu.VMEM((bsz, d, LANES), F32), pltpu.VMEM((bsz, SUBLANES, n), F32)],
        compiler_params=_cparams(("arbitrary", "arbitrary"), 48),
        name="adaln_mod",
    )(c.T, ada_w, ada_b.reshape(n_layers, 1, n))
    return out[:, :bsz]


def _log_sigmoid(x):
    return jnp.minimum(x, 0.0) - jnp.log(1.0 + jnp.exp(-jnp.abs(x)))


def _rows_kernel(*refs, has_post, has_pre, has_forget, tiles_per_seq):
    it = iter(refs)
    x_ref = next(it)
    if has_post:
        y_ref, gpost_ref, gate_ref = next(it), next(it), next(it)
    if has_pre:
        gpre_ref, sc_ref, sh_ref = next(it), next(it), next(it)
    if has_forget:
        wf_ref, fb_ref, tri_ref = next(it), next(it), next(it)
    if has_post:
        xo_ref = next(it)
    if has_pre:
        h_ref = next(it)
    if has_forget:
        c3_ref, carry_ref = next(it), next(it)

    x = x_ref[...]
    if has_post:
        y = y_ref[...].astype(F32)
        r = lax.rsqrt(jnp.mean(y * y, axis=-1, keepdims=True) + EPS)
        x = x + gate_ref[0] * (y * r * gpost_ref[...])
        xo_ref[...] = x
    if not has_pre:
        return
    r = lax.rsqrt(jnp.mean(x * x, axis=-1, keepdims=True) + EPS)
    hb = (x * r * (gpre_ref[...] * (1.0 + sc_ref[0])) + sh_ref[0]).astype(BF16)
    h_ref[...] = hb
    if not has_forget:
        return
    logf = _log_sigmoid(jnp.dot(hb, wf_ref[...], preferred_element_type=F32) + fb_ref[...])
    tri = tri_ref[...]
    local = sum(jnp.dot(tri, p, preferred_element_type=F32) for p in _split3(logf))

    @pl.when(pl.program_id(0) % tiles_per_seq == 0)
    def _():
        carry_ref[...] = jnp.zeros_like(carry_ref)

    cum = carry_ref[...] + local
    bm = cum.shape[0]
    carry_ref[...] = cum[bm - 1:bm, :]
    n1, n2, n3 = _split3(cum * (-LOG2E))
    lane = lax.broadcasted_iota(jnp.int32, cum.shape, 1)
    zero = jnp.zeros_like(n1)
    c3_ref[0] = jnp.where(lane < N_HEADS, n1,
                          jnp.where(lane < 2 * N_HEADS, n2, jnp.where(lane < 3 * N_HEADS, n3, zero)))


def _rows(x2, seq, post=None, pre=None, forget=None):
    m, d = x2.shape
    bsz = m // seq
    bm = min(256, seq)
    per_seq = seq // bm
    row = pl.BlockSpec((bm, d), lambda i: (i, 0))
    vec = pl.BlockSpec((1, d), lambda i: (0, 0))
    per_batch = pl.BlockSpec((1, 1, d), lambda i: (i // per_seq, 0, 0))
    args, in_specs, out_specs, out_shape, scratch = [x2], [row], [], [], []
    if post is not None:
        y, g_post, gate = post
        args += [y, g_post.reshape(1, d), gate]
        in_specs += [row, vec, per_batch]
    if pre is not None:
        g_pre, scale, shift = pre
        args += [g_pre.reshape(1, d), scale, shift]
        in_specs += [vec, per_batch, per_batch]
    if forget is not None:
        wf, fb = forget
        tri = (lax.broadcasted_iota(jnp.int32, (bm, bm), 1) <= lax.broadcasted_iota(jnp.int32, (bm, bm), 0)).astype(BF16)
        args += [wf, fb, tri]
        in_specs += [pl.BlockSpec((d, LANES), lambda i: (0, 0)), pl.BlockSpec((1, LANES), lambda i: (0, 0)),
                     pl.BlockSpec((bm, bm), lambda i: (0, 0))]
    if post is not None:
        out_specs.append(row)
        out_shape.append(jax.ShapeDtypeStruct((m, d), F32))
    if pre is not None:
        out_specs.append(row)
        out_shape.append(jax.ShapeDtypeStruct((m, d), BF16))
    if forget is not None:
        out_specs.append(pl.BlockSpec((1, bm, LANES), lambda i: (i // per_seq, i % per_seq, 0)))
        out_shape.append(jax.ShapeDtypeStruct((bsz, seq, LANES), BF16))
        scratch.append(pltpu.VMEM((1, LANES), F32))
    return pl.pallas_call(
        functools.partial(_rows_kernel, has_post=post is not None, has_pre=pre is not None,
                          has_forget=forget is not None, tiles_per_seq=per_seq),
        grid=(m // bm,),
        in_specs=in_specs, out_specs=out_specs, out_shape=out_shape, scratch_shapes=scratch,
        compiler_params=_cparams(("arbitrary",), 48),
        name="rows_norm",
    )(*args)


def _mm_kernel(*refs, n_a, relu2, has_side, w_t):
    a_refs, w_ref = refs[:n_a], refs[n_a]
    if has_side:
        side_in_ref, o_ref, side_out_ref = refs[n_a + 1:n_a + 4]
        side_out_ref[...] = side_in_ref[...].astype(BF16)
    else:
        o_ref = refs[n_a + 1]
    if w_t:
        assert n_a == 1
        acc = lax.dot_general(a_refs[0][...], w_ref[0], (((1,), (1,)), ((), ())), preferred_element_type=F32)
    else:
        ksz = w_ref.shape[1] // n_a
        acc = None
        for t, a_ref in enumerate(a_refs):
            part = jnp.dot(a_ref[...], w_ref[0, t * ksz:(t + 1) * ksz, :], preferred_element_type=F32)
            acc = part if acc is None else acc + part
    if relu2:
        acc = jnp.square(jnp.maximum(acc, 0.0))
    o_ref[...] = acc.astype(o_ref.dtype)


def _matmul(a_parts, w_all, layer, out_dtype, relu2=False, side=None, w_t=False):
    m = a_parts[0].shape[0]
    if w_t:
        _, n, k = w_all.shape
    else:
        _, k, n = w_all.shape
    ka = k // len(a_parts)
    bm = min(1024, m)
    bn = min(1024, n)
    nn = n // bn
    in_specs = [pl.BlockSpec((bm, ka), lambda i, j: (i, 0)) for _ in a_parts]
    if w_t:
        in_specs.append(pl.BlockSpec((1, bn, k), lambda i, j: (layer, j, 0)))
    else:
        in_specs.append(pl.BlockSpec((1, k, bn), lambda i, j: (layer, 0, j)))
    out_specs = pl.BlockSpec((bm, bn), lambda i, j: (i, j))
    out_shape = jax.ShapeDtypeStruct((m, n), out_dtype)
    args = list(a_parts) + [w_all]
    if side is not None:
        src, src_layer = side
        _, rows, cols = src.shape
        n_slabs = 1
        while n_slabs * 2 <= (m // bm) * nn and rows % (n_slabs * 2) == 0 and rows // (n_slabs * 2) >= 2 * SUBLANES:
            n_slabs *= 2
        slab = rows // n_slabs
        slab_idx = lambda i, j: jnp.minimum(i * nn + j, n_slabs - 1)
        in_specs.append(pl.BlockSpec((1, slab, cols), lambda i, j: (src_layer, slab_idx(i, j), 0)))
        out_specs = [out_specs, pl.BlockSpec((1, slab, cols), lambda i, j: (0, slab_idx(i, j), 0))]
        out_shape = [out_shape, jax.ShapeDtypeStruct((1, rows, cols), BF16)]
        args.append(src)
    return pl.pallas_call(
        functools.partial(_mm_kernel, n_a=len(a_parts), relu2=relu2, has_side=side is not None, w_t=w_t),
        grid=(m // bm, nn),
        in_specs=in_specs,
        out_specs=out_specs,
        out_shape=out_shape,
        compiler_params=_cparams(("arbitrary", "arbitrary"), 56),
        name="matmul_fullk",
    )(*args)


def _mm_acc_kernel(a_ref, w_ref, o_ref, acc_ref):
    q = pl.program_id(2)
    y = jnp.dot(a_ref[...], w_ref[0], preferred_element_type=F32)

    @pl.when(q == 0)
    def _():
        acc_ref[...] = y

    @pl.when((q != 0) & (q != pl.num_programs(2) - 1))
    def _():
        acc_ref[...] += y

    @pl.when(q == pl.num_programs(2) - 1)
    def _():
        o_ref[...] = (acc_ref[...] + y).astype(o_ref.dtype)


def _matmul_ksplit(a, w_all, layer, out_dtype):
    m, k = a.shape
    n = w_all.shape[2]
    bm = min(1024, m)
    bn = min(1024, n)
    bk = min(4096, k // 2)
    return pl.pallas_call(
        _mm_acc_kernel,
        grid=(m // bm, n // bn, k // bk),
        in_specs=[
            pl.BlockSpec((bm, bk), lambda i, j, q: (i, q)),
            pl.BlockSpec((1, bk, bn), lambda i, j, q: (layer, q, j)),
        ],
        out_specs=pl.BlockSpec((bm, bn), lambda i, j, q: (i, j)),
        out_shape=jax.ShapeDtypeStruct((m, n), out_dtype),
        scratch_shapes=[pltpu.VMEM((bm, bn), F32)],
        compiler_params=_cparams(("arbitrary", "arbitrary", "arbitrary"), 52),
        name="matmul_ksplit",
    )(a, w_all)


def _layer_norm_rows(h, g, b):
    mu = jnp.mean(h, axis=-1, keepdims=True)
    hc = h - mu
    var = jnp.mean(hc * hc, axis=-1, keepdims=True)
    return hc * lax.rsqrt(var + EPS) * g + b


def _conv_kernel(a_ref, b_ref, ah_ref, bh_ref, w_ref, cb_ref, lg_ref, lb_ref, o_ref, buf_ref, acc_ref, *, bs):
    i = pl.program_id(1)
    glu = a_ref[0].astype(F32) * jax.nn.sigmoid(b_ref[0].astype(F32))
    halo = ah_ref[0].astype(F32) * jax.nn.sigmoid(bh_ref[0].astype(F32))
    halo = jnp.where(i == 0, 0.0, halo)
    n_cb = D_GROUP // LANES
    for cb in range(n_cb):
        cs = slice(cb * LANES, (cb + 1) * LANES)
        buf_ref[cb, 0:CONV_HALO, :] = halo[:, cs]
        buf_ref[cb, CONV_HALO:CONV_HALO + bs, :] = glu[:, cs]
    first = CONV_HALO - (CONV_WIDTH - 1)
    n_u = bs // SUBLANES

    def per_col_block(cb, carry):
        taps = [jnp.broadcast_to(w_ref[cb, j:j + 1, :], (SUBLANES, LANES)) for j in range(CONV_WIDTH)]
        acc = [jnp.broadcast_to(cb_ref[cb], (SUBLANES, LANES)) for _ in range(n_u)]
        for r in range(SUBLANES):
            js = [j for j in range(CONV_WIDTH) if (first + j) % SUBLANES == r]
            ks = [(first + j) // SUBLANES for j in js]
            for v in range(min(ks), n_u + max(ks)):
                window = buf_ref[cb, r + SUBLANES * v:r + SUBLANES * (v + 1), :]
                for j, k in zip(js, ks):
                    if 0 <= v - k < n_u:
                        acc[v - k] = acc[v - k] + taps[j] * window
        for u in range(n_u):
            acc_ref[cb, u * SUBLANES:(u + 1) * SUBLANES, :] = acc[u]
        return carry

    lax.fori_loop(0, n_cb, per_col_block, 0)
    h = jnp.concatenate([acc_ref[cb] for cb in range(n_cb)], axis=1)
    h = _layer_norm_rows(h, lg_ref[...], lb_ref[...])
    o_ref[0] = (h * jax.nn.sigmoid(h)).astype(BF16)


def _conv_module(z_lo, conv_w, conv_b, ln_g, ln_b):
    bsz, seq, _ = z_lo.shape
    bs = min(128, seq)
    hb = bs // CONV_HALO
    vec = lambda v: v.reshape(1, D_GROUP)
    tile = lambda col: pl.BlockSpec((1, bs, D_GROUP), lambda b, i: (b, i, col))
    halo = lambda col: pl.BlockSpec((1, CONV_HALO, D_GROUP), lambda b, i: (b, jnp.maximum(i * hb - 1, 0), col))
    full = lambda r: pl.BlockSpec((r, D_GROUP), lambda b, i: (0, 0))
    n_cb = D_GROUP // LANES
    per_cb = lambda rows: pl.BlockSpec((n_cb, rows, LANES), lambda b, i: (0, 0, 0))
    w_cb = conv_w.reshape(CONV_WIDTH, n_cb, LANES).transpose(1, 0, 2)
    return pl.pallas_call(
        functools.partial(_conv_kernel, bs=bs),
        grid=(bsz, seq // bs),
        in_specs=[tile(ZLO_GLU_A // D_GROUP), tile(ZLO_GLU_B // D_GROUP),
                  halo(ZLO_GLU_A // D_GROUP), halo(ZLO_GLU_B // D_GROUP),
                  per_cb(CONV_WIDTH), per_cb(1), full(1), full(1)],
        out_specs=pl.BlockSpec((1, bs, D_GROUP), lambda b, i: (b, i, 0)),
        out_shape=jax.ShapeDtypeStruct((bsz, seq, D_GROUP), BF16),
        scratch_shapes=[pltpu.VMEM((n_cb, CONV_HALO + bs, LANES), F32),
                        pltpu.VMEM((n_cb, bs, LANES), F32)],
        compiler_params=_cparams(("arbitrary", "arbitrary"), 32),
        name="conv_module",
    )(z_lo, z_lo, z_lo, z_lo, w_cb, conv_b.reshape(n_cb, 1, LANES), vec(ln_g), vec(ln_b))


def _fox_kernel(q_ref, k_ref, v_ref, c3_ref, o_ref, ka_ref, va_ref, s0_ref, s1_ref, p_ref, m_ref, acc_ref,
                *, blk, sub, rg):
    h = pl.program_id(1)
    i = pl.program_id(2)

    @pl.when(i == 0)
    def _():
        ka_ref[:, :HEAD_DIM] = k_ref[0]
        ka_ref[:, HEAD_DIM:] = c3_ref[0]
        va_ref[:, :HEAD_DIM] = v_ref[0]
        lane = lax.broadcasted_iota(jnp.int32, (va_ref.shape[0], HEAD_DIM), 1)
        va_ref[:, HEAD_DIM:] = (lane == 0).astype(BF16)

    lane = lax.broadcasted_iota(jnp.int32, (blk, HEAD_DIM), 1)
    pick = (lane == h) | (lane == h + N_HEADS) | (lane == h + 2 * N_HEADS)
    qa = jnp.concatenate([q_ref[0], pick.astype(BF16)], axis=1)
    m_ref[...] = jnp.full(m_ref.shape, -jnp.inf, F32)
    acc_ref[...] = jnp.zeros(acc_ref.shape, F32)

    def logits_to(s_ref, jsub, r0=0):
        start = pl.multiple_of(jsub * sub, sub)
        s_ref[r0:, :] = lax.dot_general(qa[r0:], ka_ref[pl.ds(start, sub), :], (((1,), (1,)), ((), ())),
                                        preferred_element_type=F32)

    def update(s_ref, jsub, mask_off=None, r0=0):
        for r in range(r0 // rg, blk // rg):
            rows = slice(r * rg, (r + 1) * rg)
            s = s_ref[rows, :]
            if mask_off is not None and r * rg < mask_off + sub - 1:
                row = lax.broadcasted_iota(jnp.int32, (rg, sub), 0) + r * rg
                col = lax.broadcasted_iota(jnp.int32, (rg, sub), 1) + mask_off
                s = jnp.where(col <= row, s, -jnp.inf)
            m_old = m_ref[rows, :]
            m_new = jnp.maximum(m_old, jnp.max(s, axis=-1, keepdims=True))
            m_ref[rows, :] = m_new
            alpha = jnp.exp2(m_old - m_new)
            for cg in range(sub // LANES):
                cs = slice(cg * LANES, (cg + 1) * LANES)
                p_ref[rows, cs] = jnp.exp2(s[:, cs] - m_new).astype(BF16)
            for cg in range(acc_ref.shape[1] // LANES):
                cs = slice(cg * LANES, (cg + 1) * LANES)
                acc_ref[rows, cs] = alpha * acc_ref[rows, cs]
        start = pl.multiple_of(jsub * sub, sub)
        acc_ref[r0:, :] += jnp.dot(p_ref[r0:, :], va_ref[pl.ds(start, sub), :], preferred_element_type=F32)

    per = blk // sub
    logits_to(s0_ref, 0)

    def body(t, carry):
        logits_to(s1_ref, per * t + 1)
        update(s0_ref, per * t)
        logits_to(s0_ref, per * t + 2)
        update(s1_ref, per * t + 1)
        return carry

    lax.fori_loop(0, i, body, 0)
    logits_to(s1_ref, per * i + 1, r0=sub)
    update(s0_ref, per * i, mask_off=0)
    update(s1_ref, per * i + 1, mask_off=sub, r0=sub)
    acc = acc_ref[...]
    o_ref[0] = (acc[:, :HEAD_DIM] * (1.0 / acc[:, HEAD_DIM:HEAD_DIM + 1])).astype(BF16)


def _forgetting_attention(z_lo, c3):
    bsz, seq, _ = z_lo.shape
    blk = min(2048, seq)
    sub = blk // 2
    rg = min(128, blk)
    col = lambda off: off // HEAD_DIM
    kv = lambda off: pl.BlockSpec((1, seq, HEAD_DIM), lambda b, h, i: (b, 0, col(off) + h))
    return pl.pallas_call(
        functools.partial(_fox_kernel, blk=blk, sub=sub, rg=rg),
        grid=(bsz, N_HEADS, seq // blk),
        in_specs=[pl.BlockSpec((1, blk, HEAD_DIM), lambda b, h, i: (b, i, col(ZLO_FQ) + h)),
                  kv(ZLO_FK), kv(ZLO_FV),
                  pl.BlockSpec((1, seq, LANES), lambda b, h, i: (b, 0, 0))],
        out_specs=pl.BlockSpec((1, blk, HEAD_DIM), lambda b, h, i: (b, i, h)),
        out_shape=jax.ShapeDtypeStruct((bsz, seq, D_GROUP), BF16),
        scratch_shapes=[pltpu.VMEM((seq, 2 * HEAD_DIM), BF16), pltpu.VMEM((seq, 2 * HEAD_DIM), BF16),
                        pltpu.VMEM((blk, sub), F32), pltpu.VMEM((blk, sub), F32), pltpu.VMEM((blk, sub), BF16),
                        pltpu.VMEM((blk, LANES), F32),
                        pltpu.VMEM((blk, 2 * HEAD_DIM), F32)],
        compiler_params=_cparams(("arbitrary", "arbitrary", "arbitrary"), 56),
        name="forgetting_attention",
    )(z_lo, z_lo, z_lo, c3)


def _ret_kernel(q_ref, k_ref, v_ref, g_ref, cos_ref, sin_ref, dec_ref, qw_ref, kw_ref, cd_ref, bd_ref,
                o_ref, state_ref, *, chunk):
    n = pl.program_id(1)

    @pl.when(n == 0)
    def _():
        state_ref[...] = jnp.zeros_like(state_ref)

    cos = cos_ref[...]
    sin = sin_ref[...]
    lane = lax.broadcasted_iota(jnp.int32, (chunk, LANES), 1)
    first_half = (lane & (RET_K_DIM // 2)) == 0
    low_head = lane < RET_K_DIM

    def rotate(x):
        swapped = jnp.where(first_half, pltpu.roll(x, LANES - RET_K_DIM // 2, 1), pltpu.roll(x, RET_K_DIM // 2, 1))
        return x * cos + swapped * sin

    for pr in range(N_HEADS // 2):
        ls = slice(pr * LANES, (pr + 1) * LANES)
        qr = rotate(q_ref[0, :, ls].astype(F32))
        kr = rotate(k_ref[0, :, ls].astype(F32))
        kb = kr.astype(BF16)
        v_pair = v_ref[0, :, pr * 2 * HEAD_DIM:(pr + 1) * 2 * HEAD_DIM]
        state = state_ref[pr]
        cross = jnp.dot((qr * qw_ref[:, ls]).astype(BF16), state.astype(BF16), preferred_element_type=F32)
        for a in range(2):
            head = 2 * pr + a
            hs = slice(head * HEAD_DIM, (head + 1) * HEAD_DIM)
            qa = jnp.where(low_head if a == 0 else jnp.logical_not(low_head), qr, 0.0).astype(BF16)
            scores = lax.dot_general(qa, kb, (((1,), (1,)), ((), ())), preferred_element_type=F32) * dec_ref[head]
            inner = jnp.dot(scores.astype(BF16), v_pair[:, a * HEAD_DIM:(a + 1) * HEAD_DIM],
                            preferred_element_type=F32)
            y = inner + cross[:, a * HEAD_DIM:(a + 1) * HEAD_DIM]
            mu = jnp.mean(y, axis=-1, keepdims=True)
            yc = y - mu
            var = jnp.mean(yc * yc, axis=-1, keepdims=True)
            yn = yc * lax.rsqrt(var + GN_EPS)
            gate = g_ref[0, :, hs].astype(F32)
            o_ref[0, :, hs] = (gate * jax.nn.sigmoid(gate) * yn).astype(BF16)
        kw_t = (kr * kw_ref[:, ls]).T.astype(BF16)
        kv = jnp.dot(kw_t, v_pair, preferred_element_type=F32)
        state_ref[pr] = state * cd_ref[pr] + kv * bd_ref[...]


def _retention_tables(seq, chunk):
    half = RET_K_DIM // 2
    inv = 1.0 / (ROPE_BASE ** jnp.linspace(0.0, 1.0, half, dtype=F32))
    ang = jnp.arange(seq).astype(F32)[:, None] * inv[None, :]
    cos = jnp.tile(jnp.cos(ang), (1, LANES // half))
    sin = jnp.tile(jnp.concatenate([-jnp.sin(ang), jnp.sin(ang)], axis=-1), (1, LANES // RET_K_DIM))
    log_gamma = jnp.log(1.0 - 2.0 ** (-5.0 - jnp.arange(N_HEADS, dtype=F32)))
    idx = jnp.arange(chunk, dtype=F32)
    rel = idx[:, None] - idx[None, :]
    decay = jnp.where(rel >= 0, jnp.exp(log_gamma[:, None, None] * jnp.maximum(rel, 0.0)), 0.0)
    k_w = jnp.exp(log_gamma[:, None] * (chunk - 1.0 - idx)[None, :])
    q_w = jnp.exp(log_gamma[:, None] * (idx + 1.0)[None, :])
    per_lane = lambda t: jnp.repeat(t.T, RET_K_DIM, axis=1)
    chunk_decay = jnp.exp(log_gamma * chunk)
    cd = jnp.broadcast_to(jnp.repeat(chunk_decay, RET_K_DIM).reshape(N_HEADS // 2, LANES, 1),
                          (N_HEADS // 2, LANES, 2 * HEAD_DIM))
    r = jnp.arange(LANES)[:, None] // RET_K_DIM
    c = jnp.arange(2 * HEAD_DIM)[None, :] // HEAD_DIM
    bd = (r == c).astype(F32)
    k_scale = RET_K_DIM ** -0.5
    return cos, sin, decay * k_scale, per_lane(q_w), per_lane(k_w) * k_scale, cd, bd


def _retention(z_hi):
    bsz, seq, _ = z_hi.shape
    chunk = min(256, seq)
    cos, sin, decay, q_w, k_w, cd, bd = _retention_tables(seq, chunk)
    d_qk = N_HEADS * RET_K_DIM
    const = lambda shape: pl.BlockSpec(shape, lambda b, n: (0,) * len(shape))
    return pl.pallas_call(
        functools.partial(_ret_kernel, chunk=chunk),
        grid=(bsz, seq // chunk),
        in_specs=[pl.BlockSpec((1, chunk, d_qk), lambda b, n: (b, n, ZHI_RQ // d_qk)),
                  pl.BlockSpec((1, chunk, d_qk), lambda b, n: (b, n, ZHI_RK // d_qk)),
                  pl.BlockSpec((1, chunk, D_GROUP), lambda b, n: (b, n, ZHI_RV // D_GROUP)),
                  pl.BlockSpec((1, chunk, D_GROUP), lambda b, n: (b, n, ZHI_RG // D_GROUP)),
                  pl.BlockSpec((chunk, LANES), lambda b, n: (n, 0)),
                  pl.BlockSpec((chunk, LANES), lambda b, n: (n, 0)),
                  const((N_HEADS, chunk, chunk)), const((chunk, d_qk)), const((chunk, d_qk)),
                  const((N_HEADS // 2, LANES, 2 * HEAD_DIM)), const((LANES, 2 * HEAD_DIM))],
        out_specs=pl.BlockSpec((1, chunk, D_GROUP), lambda b, n: (b, n, 0)),
        out_shape=jax.ShapeDtypeStruct((bsz, seq, D_GROUP), BF16),
        scratch_shapes=[pltpu.VMEM((N_HEADS // 2, LANES, 2 * HEAD_DIM), F32)],
        compiler_params=_cparams(("arbitrary", "arbitrary"), 32),
        name="retention",
    )(z_hi, z_hi, z_hi, z_hi, cos, sin, decay, q_w, k_w, cd, bd)


def _gelu_tanh(x):
    return 0.5 * x * (1.0 + jnp.tanh(math.sqrt(2.0 / math.pi) * (x + 0.044715 * (x * x * x))))


def _sg_kernel(u_ref, v_ref, w_ref, bt_ref, lg_ref, lb_ref, o_ref, *, bs):
    u = _gelu_tanh(u_ref[0].astype(F32))
    v = _layer_norm_rows(_gelu_tanh(v_ref[0].astype(F32)), lg_ref[...], lb_ref[...]).astype(BF16)
    row = lax.broadcasted_iota(jnp.int32, (SG_CHUNK, SG_CHUNK), 0)
    col = lax.broadcasted_iota(jnp.int32, (SG_CHUNK, SG_CHUNK), 1)
    causal = col <= row
    for g in range(N_HEADS):
        cs = slice(g * HEAD_DIM, (g + 1) * HEAD_DIM)
        w_g = jnp.where(causal, w_ref[g], 0.0).astype(BF16)
        bias = bt_ref[:, g:g + 1]
        for r in range(bs // SG_CHUNK):
            rs = slice(r * SG_CHUNK, (r + 1) * SG_CHUNK)
            mixed = jnp.dot(w_g, v[rs, cs], preferred_element_type=F32) + bias
            o_ref[0, rs, cs] = (u[rs, cs] * mixed).astype(BF16)


def _spatial_gating(z_hi, sg_w, sg_b, ln_g, ln_b):
    bsz, seq, _ = z_hi.shape
    bs = min(512, seq)
    vec = lambda t: t.reshape(1, D_GROUP)
    tile = lambda off: pl.BlockSpec((1, bs, D_GROUP), lambda b, i: (b, i, off // D_GROUP))
    return pl.pallas_call(
        functools.partial(_sg_kernel, bs=bs),
        grid=(bsz, seq // bs),
        in_specs=[tile(ZHI_SGU), tile(ZHI_SGV),
                  pl.BlockSpec((N_HEADS, SG_CHUNK, SG_CHUNK), lambda b, i: (0, 0, 0)),
                  pl.BlockSpec((SG_CHUNK, N_HEADS), lambda b, i: (0, 0)),
                  pl.BlockSpec((1, D_GROUP), lambda b, i: (0, 0)),
                  pl.BlockSpec((1, D_GROUP), lambda b, i: (0, 0))],
        out_specs=pl.BlockSpec((1, bs, D_GROUP), lambda b, i: (b, i, 0)),
        out_shape=jax.ShapeDtypeStruct((bsz, seq, D_GROUP), BF16),
        compiler_params=_cparams(("arbitrary", "arbitrary"), 32),
        name="spatial_gating",
    )(z_hi, z_hi, sg_w, sg_b.T, vec(ln_g), vec(ln_b))


def _forget_weights(w_gate_l, f_bias_l):
    d = w_gate_l.shape[0]
    wf = jnp.concatenate([w_gate_l, w_gate_l, w_gate_l, jnp.zeros((d, LANES - 3 * N_HEADS), F32)], axis=1).astype(BF16)
    fb = jnp.concatenate([f_bias_l, f_bias_l, f_bias_l, jnp.zeros((LANES - 3 * N_HEADS,), F32)]).reshape(1, LANES)
    return wf, fb


def kernel(x, c, ada_w, ada_b, mix_pre_g, mix_post_g, w_in, fox_f_bias, conv_w, conv_b, conv_ln_g, conv_ln_b,
           sg_w, sg_b, sg_ln_g, sg_ln_b, w_out, ffn_pre_g, ffn_post_g, w_ff1, w_ff2):
    bsz, seq, d = x.shape
    n_layers = ada_w.shape[0]
    m = bsz * seq
    mod = _modulation(c, ada_w, ada_b).reshape(n_layers, bsz, 6, 1, d)
    shift1, scale1, gate1, shift2, scale2, gate2 = [mod[:, :, t] for t in range(6)]

    w_in_t = jnp.transpose(w_in, (0, 2, 1))
    q_scale = jnp.ones((FF_OFF, 1), F32).at[ZLO_FQ:ZLO_FQ + D_GROUP].set(FOX_SCALE)
    w_lo_b = (w_in_t[:, :FF_OFF, :] * q_scale).astype(BF16)
    w_hi_b = w_in_t[:, FF_OFF + N_HEADS:, :].astype(BF16)
    w_gate = w_in[:, :, FF_OFF:FF_OFF + N_HEADS]

    x2 = x.reshape(m, d)
    h, c3 = _rows(x2, seq, pre=(mix_pre_g[0], scale1[0], shift1[0]),
                  forget=_forget_weights(w_gate[0], fox_f_bias[0]))
    for l in range(n_layers):
        z_lo, w_out_b = _matmul([h], w_lo_b, l, BF16, side=(w_out, l), w_t=True)
        z_hi, w_ff1_b = _matmul([h], w_hi_b, l, BF16, side=(w_ff1, l), w_t=True)
        z_lo = z_lo.reshape(bsz, seq, D_ZHALF)
        z_hi = z_hi.reshape(bsz, seq, D_ZHALF)
        y_conv = _conv_module(z_lo, conv_w[l], conv_b[l], conv_ln_g[l], conv_ln_b[l])
        y_fox = _forgetting_attention(z_lo, c3)
        y_ret = _retention(z_hi)
        y_sg = _spatial_gating(z_hi, sg_w[l], sg_b[l], sg_ln_g[l], sg_ln_b[l])
        parts = [t.reshape(m, D_GROUP) for t in (y_conv, y_fox, y_ret, y_sg)]
        y = _matmul(parts, w_out_b, 0, BF16)
        x2, h = _rows(x2, seq, post=(y, mix_post_g[l], gate1[l]), pre=(ffn_pre_g[l], scale2[l], shift2[l]))
        u, w_ff2_b = _matmul([h], w_ff1_b, 0, BF16, relu2=True, side=(w_ff2, l))
        y = _matmul_ksplit(u, w_ff2_b, 0, BF16)
        if l + 1 < n_layers:
            x2, h, c3 = _rows(x2, seq, post=(y, ffn_post_g[l], gate2[l]),
                              pre=(mix_pre_g[l + 1], scale1[l + 1], shift1[l + 1]),
                              forget=_forget_weights(w_gate[l + 1], fox_f_bias[l + 1]))
        else:
            (x2,) = _rows(x2, seq, post=(y, ffn_post_g[l], gate2[l]))
    return x2.reshape(bsz, seq, d)
```

```python
import functools
import math

import jax
import jax.numpy as jnp
from jax import lax
from jax.experimental import pallas as pl
from jax.experimental.pallas import tpu as pltpu

F32 = jnp.float32
BF16 = jnp.bfloat16

D_GROUP = 1024
N_HEADS = 8
HEAD_DIM = D_GROUP // N_HEADS
CONV_WIDTH = 31
CONV_HALO = 32
RET_K_DIM = HEAD_DIM // 2
SG_CHUNK = 128
ROPE_BASE = 10000.0
EPS = 1e-6
GN_EPS = 1e-5
LOG2E = math.log2(math.e)
FOX_SCALE = LOG2E * HEAD_DIM ** -0.5

D_ZHALF = 5120
FF_OFF = 5120
ZLO_GLU_A, ZLO_GLU_B, ZLO_FQ, ZLO_FK, ZLO_FV = 0, 1024, 2048, 3072, 4096
ZHI_RQ, ZHI_RK, ZHI_RV, ZHI_RG, ZHI_SGU, ZHI_SGV = 0, 512, 1024, 2048, 3072, 4096

MIB = 1024 * 1024
LANES = 128
SUBLANES = 8


def _cparams(semantics, vmem_mib):
    return pltpu.CompilerParams(dimension_semantics=semantics, vmem_limit_bytes=vmem_mib * MIB)


def _split3(x):
    p1 = x.astype(BF16)
    r1 = x - p1.astype(F32)
    p2 = r1.astype(BF16)
    p3 = (r1 - p2.astype(F32)).astype(BF16)
    return p1, p2, p3


def _mod_kernel(ct_ref, w_ref, b_ref, o_ref, sb_ref, acc_ref, *, bsz):
    q = pl.program_id(1)

    @pl.when((pl.program_id(0) == 0) & (q == 0))
    def _():
        for b in range(bsz):
            cb = ct_ref[:, b:b + 1]
            sb_ref[b] = jnp.broadcast_to(cb * jax.nn.sigmoid(cb), sb_ref.shape[1:])

    @pl.when(q == 0)
    def _():
        acc_ref[...] = jnp.zeros_like(acc_ref)

    kc, n = w_ref.shape[1], w_ref.shape[2]
    start = pl.multiple_of(q * kc, kc)
    s_rows = [sb_ref[b, pl.ds(start, kc), :] for b in range(bsz)]
    for cg in range(n // LANES):
        cs = slice(cg * LANES, (cg + 1) * LANES)
        w = w_ref[0, :, cs]
        for b in range(bsz):
            acc_ref[b, :, cs] += jnp.sum((w * s_rows[b]).reshape(kc // SUBLANES, SUBLANES, LANES), axis=0)

    @pl.when(q == pl.num_programs(1) - 1)
    def _():
        o_ref[0] = jnp.zeros(o_ref.shape[1:], F32)
        for b in range(bsz):
            o_ref[0, b:b + 1, :] = jnp.sum(acc_ref[b], axis=0, keepdims=True) + b_ref[0]


def _modulation(c, ada_w, ada_b):
    n_layers, d, n = ada_w.shape
    bsz = c.shape[0]
    rows = SUBLANES
    kc = 128
    out = pl.pallas_call(
        functools.partial(_mod_kernel, bsz=bsz),
        grid=(n_layers, d // kc),
        in_specs=[
            pl.BlockSpec((d, bsz), lambda l, q: (0, 0)),
            pl.BlockSpec((1, kc, n), lambda l, q: (l, q, 0)),
            pl.BlockSpec((1, 1, n), lambda l, q: (l, 0, 0)),
        ],
        out_specs=pl.BlockSpec((1, rows, n), lambda l, q: (l, 0, 0)),
        out_shape=jax.ShapeDtypeStruct((n_layers, rows, n), F32),
        scratch_shapes=[pltpu.VMEM((bsz, d, LANES), F32), pltpu.VMEM((bsz, SUBLANES, n), F32)],
        compiler_params=_cparams(("arbitrary", "arbitrary"), 48),
        name="adaln_mod",
    )(c.T, ada_w, ada_b.reshape(n_layers, 1, n))
    return out[:, :bsz]


def _log_sigmoid(x):
    return jnp.minimum(x, 0.0) - jnp.log(1.0 + jnp.exp(-jnp.abs(x)))


def _rows_kernel(*refs, has_post, has_pre, has_forget, tiles_per_seq):
    it = iter(refs)
    x_ref = next(it)
    if has_post:
        y_ref, gpost_ref, gate_ref = next(it), next(it), next(it)
    if has_pre:
        gpre_ref, sc_ref, sh_ref = next(it), next(it), next(it)
    if has_forget:
        wf_ref, fb_ref, tri_ref = next(it), next(it), next(it)
    if has_post:
        xo_ref = next(it)
    if has_pre:
        h_ref = next(it)
    if has_forget:
        c3_ref, carry_ref = next(it), next(it)

    x = x_ref[...]
    if has_post:
        y = y_ref[...].astype(F32)
        r = lax.rsqrt(jnp.mean(y * y, axis=-1, keepdims=True) + EPS)
        x = x + gate_ref[0] * (y * r * gpost_ref[...])
        xo_ref[...] = x
    if not has_pre:
        return
    r = lax.rsqrt(jnp.mean(x * x, axis=-1, keepdims=True) + EPS)
    hb = (x * r * (gpre_ref[...] * (1.0 + sc_ref[0])) + sh_ref[0]).astype(BF16)
    h_ref[...] = hb
    if not has_forget:
        return
    logf = _log_sigmoid(jnp.dot(hb, wf_ref[...], preferred_element_type=F32) + fb_ref[...])
    tri = tri_ref[...]
    local = sum(jnp.dot(tri, p, preferred_element_type=F32) for p in _split3(logf))

    @pl.when(pl.program_id(0) % tiles_per_seq == 0)
    def _():
        carry_ref[...] = jnp.zeros_like(carry_ref)

    cum = carry_ref[...] + local
    bm = cum.shape[0]
    carry_ref[...] = cum[bm - 1:bm, :]
    n1, n2, n3 = _split3(cum * (-LOG2E))
    lane = lax.broadcasted_iota(jnp.int32, cum.shape, 1)
    zero = jnp.zeros_like(n1)
    c3_ref[0] = jnp.where(lane < N_HEADS, n1,
                          jnp.where(lane < 2 * N_HEADS, n2, jnp.where(lane < 3 * N_HEADS, n3, zero)))


def _rows(x2, seq, post=None, pre=None, forget=None):
    m, d = x2.shape
    bsz = m // seq
    bm = min(256, seq)
    per_seq = seq // bm
    row = pl.BlockSpec((bm, d), lambda i: (i, 0))
    vec = pl.BlockSpec((1, d), lambda i: (0, 0))
    per_batch = pl.BlockSpec((1, 1, d), lambda i: (i // per_seq, 0, 0))
    args, in_specs, out_specs, out_shape, scratch = [x2], [row], [], [], []
    if post is not None:
        y, g_post, gate = post
        args += [y, g_post.reshape(1, d), gate]
        in_specs += [row, vec, per_batch]
    if pre is not None:
        g_pre, scale, shift = pre
        args += [g_pre.reshape(1, d), scale, shift]
        in_specs += [vec, per_batch, per_batch]
    if forget is not None:
        wf, fb = forget
        tri = (lax.broadcasted_iota(jnp.int32, (bm, bm), 1) <= lax.broadcasted_iota(jnp.int32, (bm, bm), 0)).astype(BF16)
        args += [wf, fb, tri]
        in_specs += [pl.BlockSpec((d, LANES), lambda i: (0, 0)), pl.BlockSpec((1, LANES), lambda i: (0, 0)),
                     pl.BlockSpec((bm, bm), lambda i: (0, 0))]
    if post is not None:
        out_specs.append(row)
        out_shape.append(jax.ShapeDtypeStruct((m, d), F32))
    if pre is not None:
        out_specs.append(row)
        out_shape.append(jax.ShapeDtypeStruct((m, d), BF16))
    if forget is not None:
        out_specs.append(pl.BlockSpec((1, bm, LANES), lambda i: (i // per_seq, i % per_seq, 0)))
        out_shape.append(jax.ShapeDtypeStruct((bsz, seq, LANES), BF16))
        scratch.append(pltpu.VMEM((1, LANES), F32))
    return pl.pallas_call(
        functools.partial(_rows_kernel, has_post=post is not None, has_pre=pre is not None,
                          has_forget=forget is not None, tiles_per_seq=per_seq),
        grid=(m // bm,),
        in_specs=in_specs, out_specs=out_specs, out_shape=out_shape, scratch_shapes=scratch,
        compiler_params=_cparams(("arbitrary",), 48),
        name="rows_norm",
    )(*args)


def _mm_kernel(*refs, n_a, relu2, has_side, w_t):
    a_refs, w_ref = refs[:n_a], refs[n_a]
    if has_side:
        side_in_ref, o_ref, side_out_ref = refs[n_a + 1:n_a + 4]
        side_out_ref[...] = side_in_ref[...].astype(BF16)
    else:
        o_ref = refs[n_a + 1]
    if w_t:
        assert n_a == 1
        acc = lax.dot_general(a_refs[0][...], w_ref[0], (((1,), (1,)), ((), ())), preferred_element_type=F32)
    else:
        ksz = w_ref.shape[1] // n_a
        acc = None
        for t, a_ref in enumerate(a_refs):
            part = jnp.dot(a_ref[...], w_ref[0, t * ksz:(t + 1) * ksz, :], preferred_element_type=F32)
            acc = part if acc is None else acc + part
    if relu2:
        acc = jnp.square(jnp.maximum(acc, 0.0))
    o_ref[...] = acc.astype(o_ref.dtype)


def _matmul(a_parts, w_all, layer, out_dtype, relu2=False, side=None, w_t=False):
    m = a_parts[0].shape[0]
    if w_t:
        _, n, k = w_all.shape
    else:
        _, k, n = w_all.shape
    ka = k // len(a_parts)
    bm = min(1024, m)
    bn = min(1024, n)
    nn = n // bn
    in_specs = [pl.BlockSpec((bm, ka), lambda i, j: (i, 0)) for _ in a_parts]
    if w_t:
        in_specs.append(pl.BlockSpec((1, bn, k), lambda i, j: (layer, j, 0)))
    else:
        in_specs.append(pl.BlockSpec((1, k, bn), lambda i, j: (layer, 0, j)))
    out_specs = pl.BlockSpec((bm, bn), lambda i, j: (i, j))
    out_shape = jax.ShapeDtypeStruct((m, n), out_dtype)
    args = list(a_parts) + [w_all]
    if side is not None:
        src, src_layer = side
        _, rows, cols = src.shape
        n_slabs = 1
        while n_slabs * 2 <= (m // bm) * nn and rows % (n_slabs * 2) == 0 and rows // (n_slabs * 2) >= 2 * SUBLANES:
            n_slabs *= 2
        slab = rows // n_slabs
        slab_idx = lambda i, j: jnp.minimum(i * nn + j, n_slabs - 1)
        in_specs.append(pl.BlockSpec((1, slab, cols), lambda i, j: (src_layer, slab_idx(i, j), 0)))
        out_specs = [out_specs, pl.BlockSpec((1, slab, cols), lambda i, j: (0, slab_idx(i, j), 0))]
        out_shape = [out_shape, jax.ShapeDtypeStruct((1, rows, cols), BF16)]
        args.append(src)
    return pl.pallas_call(
        functools.partial(_mm_kernel, n_a=len(a_parts), relu2=relu2, has_side=side is not None, w_t=w_t),
        grid=(m // bm, nn),
        in_specs=in_specs,
        out_specs=out_specs,
        out_shape=out_shape,
        compiler_params=_cparams(("arbitrary", "arbitrary"), 56),
        name="matmul_fullk",
    )(*args)


def _mm_acc_kernel(a_ref, w_ref, o_ref, acc_ref):
    @pl.when(pl.program_id(2) == 0)
    def _():
        acc_ref[...] = jnp.zeros_like(acc_ref)

    acc = acc_ref[...] + jnp.dot(a_ref[...], w_ref[0], preferred_element_type=F32)
    acc_ref[...] = acc
    o_ref[...] = acc.astype(o_ref.dtype)


def _matmul_ksplit(a, w_all, layer, out_dtype):
    m, k = a.shape
    n = w_all.shape[2]
    bm = min(1024, m)
    bn = min(1024, n)
    bk = min(4096, k)
    return pl.pallas_call(
        _mm_acc_kernel,
        grid=(m // bm, n // bn, k // bk),
        in_specs=[
            pl.BlockSpec((bm, bk), lambda i, j, q: (i, q)),
            pl.BlockSpec((1, bk, bn), lambda i, j, q: (layer, q, j)),
        ],
        out_specs=pl.BlockSpec((bm, bn), lambda i, j, q: (i, j)),
        out_shape=jax.ShapeDtypeStruct((m, n), out_dtype),
        scratch_shapes=[pltpu.VMEM((bm, bn), F32)],
        compiler_params=_cparams(("arbitrary", "arbitrary", "arbitrary"), 52),
        name="matmul_ksplit",
    )(a, w_all)


def _layer_norm_rows(h, g, b):
    mu = jnp.mean(h, axis=-1, keepdims=True)
    hc = h - mu
    var = jnp.mean(hc * hc, axis=-1, keepdims=True)
    return hc * lax.rsqrt(var + EPS) * g + b


def _conv_kernel(a_ref, b_ref, ah_ref, bh_ref, w_ref, cb_ref, lg_ref, lb_ref, o_ref, buf_ref, acc_ref, *, bs):
    i = pl.program_id(1)
    glu = a_ref[0].astype(F32) * jax.nn.sigmoid(b_ref[0].astype(F32))
    halo = ah_ref[0].astype(F32) * jax.nn.sigmoid(bh_ref[0].astype(F32))
    halo = jnp.where(i == 0, 0.0, halo)
    n_cb = D_GROUP // LANES
    for cb in range(n_cb):
        cs = slice(cb * LANES, (cb + 1) * LANES)
        buf_ref[cb, 0:CONV_HALO, :] = halo[:, cs]
        buf_ref[cb, CONV_HALO:CONV_HALO + bs, :] = glu[:, cs]
    first = CONV_HALO - (CONV_WIDTH - 1)
    n_u = bs // SUBLANES

    def per_col_block(cb, carry):
        taps = [jnp.broadcast_to(w_ref[cb, j:j + 1, :], (SUBLANES, LANES)) for j in range(CONV_WIDTH)]
        acc = [jnp.broadcast_to(cb_ref[cb], (SUBLANES, LANES)) for _ in range(n_u)]
        for r in range(SUBLANES):
            js = [j for j in range(CONV_WIDTH) if (first + j) % SUBLANES == r]
            ks = [(first + j) // SUBLANES for j in js]
            for v in range(min(ks), n_u + max(ks)):
                window = buf_ref[cb, r + SUBLANES * v:r + SUBLANES * (v + 1), :]
                for j, k in zip(js, ks):
                    if 0 <= v - k < n_u:
                        acc[v - k] = acc[v - k] + taps[j] * window
        for u in range(n_u):
            acc_ref[cb, u * SUBLANES:(u + 1) * SUBLANES, :] = acc[u]
        return carry

    lax.fori_loop(0, n_cb, per_col_block, 0)
    h = jnp.concatenate([acc_ref[cb] for cb in range(n_cb)], axis=1)
    h = _layer_norm_rows(h, lg_ref[...], lb_ref[...])
    o_ref[0] = (h * jax.nn.sigmoid(h)).astype(BF16)


def _conv_module(z_lo, conv_w, conv_b, ln_g, ln_b):
    bsz, seq, _ = z_lo.shape
    bs = min(128, seq)
    hb = bs // CONV_HALO
    vec = lambda v: v.reshape(1, D_GROUP)
    tile = lambda col: pl.BlockSpec((1, bs, D_GROUP), lambda b, i: (b, i, col))
    halo = lambda col: pl.BlockSpec((1, CONV_HALO, D_GROUP), lambda b, i: (b, jnp.maximum(i * hb - 1, 0), col))
    full = lambda r: pl.BlockSpec((r, D_GROUP), lambda b, i: (0, 0))
    n_cb = D_GROUP // LANES
    per_cb = lambda rows: pl.BlockSpec((n_cb, rows, LANES), lambda b, i: (0, 0, 0))
    w_cb = conv_w.reshape(CONV_WIDTH, n_cb, LANES).transpose(1, 0, 2)
    return pl.pallas_call(
        functools.partial(_conv_kernel, bs=bs),
        grid=(bsz, seq // bs),
        in_specs=[tile(ZLO_GLU_A // D_GROUP), tile(ZLO_GLU_B // D_GROUP),
                  halo(ZLO_GLU_A // D_GROUP), halo(ZLO_GLU_B // D_GROUP),
                  per_cb(CONV_WIDTH), per_cb(1), full(1), full(1)],
        out_specs=pl.BlockSpec((1, bs, D_GROUP), lambda b, i: (b, i, 0)),
        out_shape=jax.ShapeDtypeStruct((bsz, seq, D_GROUP), BF16),
        scratch_shapes=[pltpu.VMEM((n_cb, CONV_HALO + bs, LANES), F32),
                        pltpu.VMEM((n_cb, bs, LANES), F32)],
        compiler_params=_cparams(("arbitrary", "arbitrary"), 32),
        name="conv_module",
    )(z_lo, z_lo, z_lo, z_lo, w_cb, conv_b.reshape(n_cb, 1, LANES), vec(ln_g), vec(ln_b))


def _fox_kernel(q_ref, k_ref, v_ref, c3_ref, o_ref, ka_ref, va_ref, s0_ref, s1_ref, p_ref, m_ref, acc_ref,
                *, blk, sub, rg):
    h = pl.program_id(1)
    i = pl.program_id(2)

    @pl.when(i == 0)
    def _():
        ka_ref[:, :HEAD_DIM] = k_ref[0]
        ka_ref[:, HEAD_DIM:] = c3_ref[0]
        va_ref[:, :HEAD_DIM] = v_ref[0]
        lane = lax.broadcasted_iota(jnp.int32, (va_ref.shape[0], HEAD_DIM), 1)
        va_ref[:, HEAD_DIM:] = (lane == 0).astype(BF16)

    lane = lax.broadcasted_iota(jnp.int32, (blk, HEAD_DIM), 1)
    pick = (lane == h) | (lane == h + N_HEADS) | (lane == h + 2 * N_HEADS)
    qa = jnp.concatenate([q_ref[0], pick.astype(BF16)], axis=1)
    m_ref[...] = jnp.full(m_ref.shape, -jnp.inf, F32)
    acc_ref[...] = jnp.zeros(acc_ref.shape, F32)

    def logits_to(s_ref, jsub, r0=0):
        start = pl.multiple_of(jsub * sub, sub)
        s_ref[r0:, :] = lax.dot_general(qa[r0:], ka_ref[pl.ds(start, sub), :], (((1,), (1,)), ((), ())),
                                        preferred_element_type=F32)

    def update(s_ref, jsub, mask_off=None, r0=0):
        for r in range(r0 // rg, blk // rg):
            rows = slice(r * rg, (r + 1) * rg)
            s = s_ref[rows, :]
            if mask_off is not None and r * rg < mask_off + sub - 1:
                row = lax.broadcasted_iota(jnp.int32, (rg, sub), 0) + r * rg
                col = lax.broadcasted_iota(jnp.int32, (rg, sub), 1) + mask_off
                s = jnp.where(col <= row, s, -jnp.inf)
            m_old = m_ref[rows, :]
            m_new = jnp.maximum(m_old, jnp.max(s, axis=-1, keepdims=True))
            m_ref[rows, :] = m_new
            alpha = jnp.exp2(m_old - m_new)
            for cg in range(sub // LANES):
                cs = slice(cg * LANES, (cg + 1) * LANES)
                p_ref[rows, cs] = jnp.exp2(s[:, cs] - m_new).astype(BF16)
            for cg in range(acc_ref.shape[1] // LANES):
                cs = slice(cg * LANES, (cg + 1) * LANES)
                acc_ref[rows, cs] = alpha * acc_ref[rows, cs]
        start = pl.multiple_of(jsub * sub, sub)
        acc_ref[r0:, :] += jnp.dot(p_ref[r0:, :], va_ref[pl.ds(start, sub), :], preferred_element_type=F32)

    per = blk // sub
    logits_to(s0_ref, 0)

    def body(t, carry):
        logits_to(s1_ref, per * t + 1)
        update(s0_ref, per * t)
        logits_to(s0_ref, per * t + 2)
        update(s1_ref, per * t + 1)
        return carry

    lax.fori_loop(0, i, body, 0)
    logits_to(s1_ref, per * i + 1, r0=sub)
    update(s0_ref, per * i, mask_off=0)
    update(s1_ref, per * i + 1, mask_off=sub, r0=sub)
    acc = acc_ref[...]
    o_ref[0] = (acc[:, :HEAD_DIM] * (1.0 / acc[:, HEAD_DIM:HEAD_DIM + 1])).astype(BF16)


def _forgetting_attention(z_lo, c3):
    bsz, seq, _ = z_lo.shape
    blk = min(2048, seq)
    sub = blk // 2
    rg = min(128, blk)
    col = lambda off: off // HEAD_DIM
    kv = lambda off: pl.BlockSpec((1, seq, HEAD_DIM), lambda b, h, i: (b, 0, col(off) + h))
    return pl.pallas_call(
        functools.partial(_fox_kernel, blk=blk, sub=sub, rg=rg),
        grid=(bsz, N_HEADS, seq // blk),
        in_specs=[pl.BlockSpec((1, blk, HEAD_DIM), lambda b, h, i: (b, i, col(ZLO_FQ) + h)),
                  kv(ZLO_FK), kv(ZLO_FV),
                  pl.BlockSpec((1, seq, LANES), lambda b, h, i: (b, 0, 0))],
        out_specs=pl.BlockSpec((1, blk, HEAD_DIM), lambda b, h, i: (b, i, h)),
        out_shape=jax.ShapeDtypeStruct((bsz, seq, D_GROUP), BF16),
        scratch_shapes=[pltpu.VMEM((seq, 2 * HEAD_DIM), BF16), pltpu.VMEM((seq, 2 * HEAD_DIM), BF16),
                        pltpu.VMEM((blk, sub), F32), pltpu.VMEM((blk, sub), F32), pltpu.VMEM((blk, sub), BF16),
                        pltpu.VMEM((blk, LANES), F32),
                        pltpu.VMEM((blk, 2 * HEAD_DIM), F32)],
        compiler_params=_cparams(("arbitrary", "arbitrary", "arbitrary"), 56),
        name="forgetting_attention",
    )(z_lo, z_lo, z_lo, c3)


def _ret_kernel(q_ref, k_ref, v_ref, g_ref, cos_ref, sin_ref, dec_ref, qw_ref, kw_ref, cd_ref, bd_ref,
                o_ref, state_ref, *, chunk):
    n = pl.program_id(1)

    @pl.when(n == 0)
    def _():
        state_ref[...] = jnp.zeros_like(state_ref)

    cos = cos_ref[...]
    sin = sin_ref[...]
    lane = lax.broadcasted_iota(jnp.int32, (chunk, LANES), 1)
    first_half = (lane & (RET_K_DIM // 2)) == 0
    low_head = lane < RET_K_DIM

    def rotate(x):
        swapped = jnp.where(first_half, pltpu.roll(x, LANES - RET_K_DIM // 2, 1), pltpu.roll(x, RET_K_DIM // 2, 1))
        return x * cos + swapped * sin

    for pr in range(N_HEADS // 2):
        ls = slice(pr * LANES, (pr + 1) * LANES)
        qr = rotate(q_ref[0, :, ls].astype(F32))
        kr = rotate(k_ref[0, :, ls].astype(F32))
        kb = kr.astype(BF16)
        v_pair = v_ref[0, :, pr * 2 * HEAD_DIM:(pr + 1) * 2 * HEAD_DIM]
        state = state_ref[pr]
        cross = jnp.dot((qr * qw_ref[:, ls]).astype(BF16), state.astype(BF16), preferred_element_type=F32)
        for a in range(2):
            head = 2 * pr + a
            hs = slice(head * HEAD_DIM, (head + 1) * HEAD_DIM)
            qa = jnp.where(low_head if a == 0 else jnp.logical_not(low_head), qr, 0.0).astype(BF16)
            scores = lax.dot_general(qa, kb, (((1,), (1,)), ((), ())), preferred_element_type=F32) * dec_ref[head]
            inner = jnp.dot(scores.astype(BF16), v_pair[:, a * HEAD_DIM:(a + 1) * HEAD_DIM],
                            preferred_element_type=F32)
            y = inner + cross[:, a * HEAD_DIM:(a + 1) * HEAD_DIM]
            mu = jnp.mean(y, axis=-1, keepdims=True)
            yc = y - mu
            var = jnp.mean(yc * yc, axis=-1, keepdims=True)
            yn = yc * lax.rsqrt(var + GN_EPS)
            gate = g_ref[0, :, hs].astype(F32)
            o_ref[0, :, hs] = (gate * jax.nn.sigmoid(gate) * yn).astype(BF16)
        kw_t = (kr * kw_ref[:, ls]).T.astype(BF16)
        kv = jnp.dot(kw_t, v_pair, preferred_element_type=F32)
        state_ref[pr] = state * cd_ref[pr] + kv * bd_ref[...]


def _retention_tables(seq, chunk):
    half = RET_K_DIM // 2
    inv = 1.0 / (ROPE_BASE ** jnp.linspace(0.0, 1.0, half, dtype=F32))
    ang = jnp.arange(seq).astype(F32)[:, None] * inv[None, :]
    cos = jnp.tile(jnp.cos(ang), (1, LANES // half))
    sin = jnp.tile(jnp.concatenate([-jnp.sin(ang), jnp.sin(ang)], axis=-1), (1, LANES // RET_K_DIM))
    log_gamma = jnp.log(1.0 - 2.0 ** (-5.0 - jnp.arange(N_HEADS, dtype=F32)))
    idx = jnp.arange(chunk, dtype=F32)
    rel = idx[:, None] - idx[None, :]
    decay = jnp.where(rel >= 0, jnp.exp(log_gamma[:, None, None] * jnp.maximum(rel, 0.0)), 0.0)
    k_w = jnp.exp(log_gamma[:, None] * (chunk - 1.0 - idx)[None, :])
    q_w = jnp.exp(log_gamma[:, None] * (idx + 1.0)[None, :])
    per_lane = lambda t: jnp.repeat(t.T, RET_K_DIM, axis=1)
    chunk_decay = jnp.exp(log_gamma * chunk)
    cd = jnp.broadcast_to(jnp.repeat(chunk_decay, RET_K_DIM).reshape(N_HEADS // 2, LANES, 1),
                          (N_HEADS // 2, LANES, 2 * HEAD_DIM))
    r = jnp.arange(LANES)[:, None] // RET_K_DIM
    c = jnp.arange(2 * HEAD_DIM)[None, :] // HEAD_DIM
    bd = (r == c).astype(F32)
    k_scale = RET_K_DIM ** -0.5
    return cos, sin, decay * k_scale, per_lane(q_w), per_lane(k_w) * k_scale, cd, bd


def _retention(z_hi):
    bsz, seq, _ = z_hi.shape
    chunk = min(256, seq)
    cos, sin, decay, q_w, k_w, cd, bd = _retention_tables(seq, chunk)
    d_qk = N_HEADS * RET_K_DIM
    const = lambda shape: pl.BlockSpec(shape, lambda b, n: (0,) * len(shape))
    return pl.pallas_call(
        functools.partial(_ret_kernel, chunk=chunk),
        grid=(bsz, seq // chunk),
        in_specs=[pl.BlockSpec((1, chunk, d_qk), lambda b, n: (b, n, ZHI_RQ // d_qk)),
                  pl.BlockSpec((1, chunk, d_qk), lambda b, n: (b, n, ZHI_RK // d_qk)),
                  pl.BlockSpec((1, chunk, D_GROUP), lambda b, n: (b, n, ZHI_RV // D_GROUP)),
                  pl.BlockSpec((1, chunk, D_GROUP), lambda b, n: (b, n, ZHI_RG // D_GROUP)),
                  pl.BlockSpec((chunk, LANES), lambda b, n: (n, 0)),
                  pl.BlockSpec((chunk, LANES), lambda b, n: (n, 0)),
                  const((N_HEADS, chunk, chunk)), const((chunk, d_qk)), const((chunk, d_qk)),
                  const((N_HEADS // 2, LANES, 2 * HEAD_DIM)), const((LANES, 2 * HEAD_DIM))],
        out_specs=pl.BlockSpec((1, chunk, D_GROUP), lambda b, n: (b, n, 0)),
        out_shape=jax.ShapeDtypeStruct((bsz, seq, D_GROUP), BF16),
        scratch_shapes=[pltpu.VMEM((N_HEADS // 2, LANES, 2 * HEAD_DIM), F32)],
        compiler_params=_cparams(("arbitrary", "arbitrary"), 32),
        name="retention",
    )(z_hi, z_hi, z_hi, z_hi, cos, sin, decay, q_w, k_w, cd, bd)


def _gelu_tanh(x):
    return 0.5 * x * (1.0 + jnp.tanh(math.sqrt(2.0 / math.pi) * (x + 0.044715 * (x * x * x))))


def _sg_kernel(u_ref, v_ref, w_ref, bt_ref, lg_ref, lb_ref, o_ref, *, bs):
    u = _gelu_tanh(u_ref[0].astype(F32))
    v = _layer_norm_rows(_gelu_tanh(v_ref[0].astype(F32)), lg_ref[...], lb_ref[...]).astype(BF16)
    row = lax.broadcasted_iota(jnp.int32, (SG_CHUNK, SG_CHUNK), 0)
    col = lax.broadcasted_iota(jnp.int32, (SG_CHUNK, SG_CHUNK), 1)
    causal = col <= row
    for g in range(N_HEADS):
        cs = slice(g * HEAD_DIM, (g + 1) * HEAD_DIM)
        w_g = jnp.where(causal, w_ref[g], 0.0).astype(BF16)
        bias = bt_ref[:, g:g + 1]
        for r in range(bs // SG_CHUNK):
            rs = slice(r * SG_CHUNK, (r + 1) * SG_CHUNK)
            mixed = jnp.dot(w_g, v[rs, cs], preferred_element_type=F32) + bias
            o_ref[0, rs, cs] = (u[rs, cs] * mixed).astype(BF16)


def _spatial_gating(z_hi, sg_w, sg_b, ln_g, ln_b):
    bsz, seq, _ = z_hi.shape
    bs = min(512, seq)
    vec = lambda t: t.reshape(1, D_GROUP)
    tile = lambda off: pl.BlockSpec((1, bs, D_GROUP), lambda b, i: (b, i, off // D_GROUP))
    return pl.pallas_call(
        functools.partial(_sg_kernel, bs=bs),
        grid=(bsz, seq // bs),
        in_specs=[tile(ZHI_SGU), tile(ZHI_SGV),
                  pl.BlockSpec((N_HEADS, SG_CHUNK, SG_CHUNK), lambda b, i: (0, 0, 0)),
                  pl.BlockSpec((SG_CHUNK, N_HEADS), lambda b, i: (0, 0)),
                  pl.BlockSpec((1, D_GROUP), lambda b, i: (0, 0)),
                  pl.BlockSpec((1, D_GROUP), lambda b, i: (0, 0))],
        out_specs=pl.BlockSpec((1, bs, D_GROUP), lambda b, i: (b, i, 0)),
        out_shape=jax.ShapeDtypeStruct((bsz, seq, D_GROUP), BF16),
        compiler_params=_cparams(("arbitrary", "arbitrary"), 32),
        name="spatial_gating",
    )(z_hi, z_hi, sg_w, sg_b.T, vec(ln_g), vec(ln_b))


def _forget_weights(w_gate_l, f_bias_l):
    d = w_gate_l.shape[0]
    wf = jnp.concatenate([w_gate_l, w_gate_l, w_gate_l, jnp.zeros((d, LANES - 3 * N_HEADS), F32)], axis=1).astype(BF16)
    fb = jnp.concatenate([f_bias_l, f_bias_l, f_bias_l, jnp.zeros((LANES - 3 * N_HEADS,), F32)]).reshape(1, LANES)
    return wf, fb


def kernel(x, c, ada_w, ada_b, mix_pre_g, mix_post_g, w_in, fox_f_bias, conv_w, conv_b, conv_ln_g, conv_ln_b,
           sg_w, sg_b, sg_ln_g, sg_ln_b, w_out, ffn_pre_g, ffn_post_g, w_ff1, w_ff2):
    bsz, seq, d = x.shape
    n_layers = ada_w.shape[0]
    m = bsz * seq
    mod = _modulation(c, ada_w, ada_b).reshape(n_layers, bsz, 6, 1, d)
    shift1, scale1, gate1, shift2, scale2, gate2 = [mod[:, :, t] for t in range(6)]

    w_in_t = jnp.transpose(w_in, (0, 2, 1))
    q_scale = jnp.ones((FF_OFF, 1), F32).at[ZLO_FQ:ZLO_FQ + D_GROUP].set(FOX_SCALE)
    w_lo_b = (w_in_t[:, :FF_OFF, :] * q_scale).astype(BF16)
    w_hi_b = w_in_t[:, FF_OFF + N_HEADS:, :].astype(BF16)
    w_gate = w_in[:, :, FF_OFF:FF_OFF + N_HEADS]

    x2 = x.reshape(m, d)
    h, c3 = _rows(x2, seq, pre=(mix_pre_g[0], scale1[0], shift1[0]),
                  forget=_forget_weights(w_gate[0], fox_f_bias[0]))
    for l in range(n_layers):
        z_lo, w_out_b = _matmul([h], w_lo_b, l, BF16, side=(w_out, l), w_t=True)
        z_hi, w_ff1_b = _matmul([h], w_hi_b, l, BF16, side=(w_ff1, l), w_t=True)
        z_lo = z_lo.reshape(bsz, seq, D_ZHALF)
        z_hi = z_hi.reshape(bsz, seq, D_ZHALF)
        y_conv = _conv_module(z_lo, conv_w[l], conv_b[l], conv_ln_g[l], conv_ln_b[l])
        y_fox = _forgetting_attention(z_lo, c3)
        y_ret = _retention(z_hi)
        y_sg = _spatial_gating(z_hi, sg_w[l], sg_b[l], sg_ln_g[l], sg_ln_b[l])
        parts = [t.reshape(m, D_GROUP) for t in (y_conv, y_fox, y_ret, y_sg)]
        y = _matmul(parts, w_out_b, 0, BF16)
        x2, h = _rows(x2, seq, post=(y, mix_post_g[l], gate1[l]), pre=(ffn_pre_g[l], scale2[l], shift2[l]))
        u, w_ff2_b = _matmul([h], w_ff1_b, 0, BF16, relu2=True, side=(w_ff2, l))
        y = _matmul_ksplit(u, w_ff2_b, 0, BF16)
        if l + 1 < n_layers:
            x2, h, c3 = _rows(x2, seq, post=(y, ffn_post_g[l], gate2[l]),
                              pre=(mix_pre_g[l + 1], scale1[l + 1], shift1[l + 1]),
                              forget=_forget_weights(w_gate[l + 1], fox_f_bias[l + 1]))
        else:
            (x2,) = _rows(x2, seq, post=(y, ffn_post_g[l], gate2[l]))
    return x2.reshape(bsz, seq, d)
```

```python
import functools
import math

import jax
import jax.numpy as jnp
from jax import lax
from jax.experimental import pallas as pl
from jax.experimental.pallas import tpu as pltpu

F32 = jnp.float32
BF16 = jnp.bfloat16

D_GROUP = 1024
N_HEADS = 8
HEAD_DIM = D_GROUP // N_HEADS
CONV_WIDTH = 31
CONV_HALO = 32
RET_K_DIM = HEAD_DIM // 2
SG_CHUNK = 128
ROPE_BASE = 10000.0
EPS = 1e-6
GN_EPS = 1e-5
LOG2E = math.log2(math.e)
FOX_SCALE = LOG2E * HEAD_DIM ** -0.5

D_ZHALF = 5120
FF_OFF = 5120
ZLO_GLU_A, ZLO_GLU_B, ZLO_FQ, ZLO_FK, ZLO_FV = 0, 1024, 2048, 3072, 4096
ZHI_RQ, ZHI_RK, ZHI_RV, ZHI_RG, ZHI_SGU, ZHI_SGV = 0, 512, 1024, 2048, 3072, 4096

MATMUL_TILE = 1024
MATMUL_K_TILE = 4096
MATMUL_VMEM = 56
ROWS_TILE = 256
ROWS_VMEM = 48
MOD_SLAB = 128
MOD_VMEM = 48
CONV_TILE = 128
FOX_BLOCK = 2048
FOX_ROW_GROUP = 128
FOX_VMEM = 56
RET_CHUNK = 256
SG_TILE = 512
MIXER_VMEM = 32

MIB = 1024 * 1024
LANES = 128
SUBLANES = 8


def _cparams(semantics, vmem_mib):
    return pltpu.CompilerParams(dimension_semantics=semantics, vmem_limit_bytes=vmem_mib * MIB)


def _split3(x):
    p1 = x.astype(BF16)
    r1 = x - p1.astype(F32)
    p2 = r1.astype(BF16)
    p3 = (r1 - p2.astype(F32)).astype(BF16)
    return p1, p2, p3


def _mod_kernel(ct_ref, w_ref, b_ref, o_ref, sb_ref, acc_ref, *, bsz):
    q = pl.program_id(1)

    @pl.when((pl.program_id(0) == 0) & (q == 0))
    def _():
        for b in range(bsz):
            cb = ct_ref[:, b:b + 1]
            sb_ref[b] = jnp.broadcast_to(cb * jax.nn.sigmoid(cb), sb_ref.shape[1:])

    @pl.when(q == 0)
    def _():
        acc_ref[...] = jnp.zeros_like(acc_ref)

    kc, n = w_ref.shape[1], w_ref.shape[2]
    start = pl.multiple_of(q * kc, kc)
    s_rows = [sb_ref[b, pl.ds(start, kc), :] for b in range(bsz)]
    for cg in range(n // LANES):
        cs = slice(cg * LANES, (cg + 1) * LANES)
        w = w_ref[0, :, cs]
        for b in range(bsz):
            acc_ref[b, :, cs] += jnp.sum((w * s_rows[b]).reshape(kc // SUBLANES, SUBLANES, LANES), axis=0)

    @pl.when(q == pl.num_programs(1) - 1)
    def _():
        o_ref[0] = jnp.zeros(o_ref.shape[1:], F32)
        for b in range(bsz):
            o_ref[0, b:b + 1, :] = jnp.sum(acc_ref[b], axis=0, keepdims=True) + b_ref[0]


def _modulation(c, ada_w, ada_b):
    n_layers, d, n = ada_w.shape
    bsz = c.shape[0]
    rows = SUBLANES
    kc = MOD_SLAB
    out = pl.pallas_call(
        functools.partial(_mod_kernel, bsz=bsz),
        grid=(n_layers, d // kc),
        in_specs=[
            pl.BlockSpec((d, bsz), lambda l, q: (0, 0)),
            pl.BlockSpec((1, kc, n), lambda l, q: (l, q, 0)),
            pl.BlockSpec((1, 1, n), lambda l, q: (l, 0, 0)),
        ],
        out_specs=pl.BlockSpec((1, rows, n), lambda l, q: (l, 0, 0)),
        out_shape=jax.ShapeDtypeStruct((n_layers, rows, n), F32),
        scratch_shapes=[pltpu.VMEM((bsz, d, LANES), F32), pltpu.VMEM((bsz, SUBLANES, n), F32)],
        compiler_params=_cparams(("arbitrary", "arbitrary"), MOD_VMEM),
        name="adaln_mod",
    )(c.T, ada_w, ada_b.reshape(n_layers, 1, n))
    return out[:, :bsz]


def _log_sigmoid(x):
    return jnp.minimum(x, 0.0) - jnp.log(1.0 + jnp.exp(-jnp.abs(x)))


def _rows_kernel(*refs, has_post, has_pre, has_forget, tiles_per_seq):
    it = iter(refs)
    x_ref = next(it)
    if has_post:
        y_ref, gpost_ref, gate_ref = next(it), next(it), next(it)
    if has_pre:
        gpre_ref, sc_ref, sh_ref = next(it), next(it), next(it)
    if has_forget:
        wf_ref, fb_ref, tri_ref = next(it), next(it), next(it)
    if has_post:
        xo_ref = next(it)
    if has_pre:
        h_ref = next(it)
    if has_forget:
        c3_ref, carry_ref = next(it), next(it)

    x = x_ref[...]
    if has_post:
        y = y_ref[...].astype(F32)
        r = lax.rsqrt(jnp.mean(y * y, axis=-1, keepdims=True) + EPS)
        x = x + gate_ref[0] * (y * r * gpost_ref[...])
        xo_ref[...] = x
    if not has_pre:
        return
    r = lax.rsqrt(jnp.mean(x * x, axis=-1, keepdims=True) + EPS)
    hb = (x * r * (gpre_ref[...] * (1.0 + sc_ref[0])) + sh_ref[0]).astype(BF16)
    h_ref[...] = hb
    if not has_forget:
        return
    logf = _log_sigmoid(jnp.dot(hb, wf_ref[...], preferred_element_type=F32) + fb_ref[...])
    tri = tri_ref[...]
    local = sum(jnp.dot(tri, p, preferred_element_type=F32) for p in _split3(logf))

    @pl.when(pl.program_id(0) % tiles_per_seq == 0)
    def _():
        carry_ref[...] = jnp.zeros_like(carry_ref)

    cum = carry_ref[...] + local
    bm = cum.shape[0]
    carry_ref[...] = cum[bm - 1:bm, :]
    n1, n2, n3 = _split3(cum * (-LOG2E))
    lane = lax.broadcasted_iota(jnp.int32, cum.shape, 1)
    zero = jnp.zeros_like(n1)
    c3_ref[0] = jnp.where(lane < N_HEADS, n1,
                          jnp.where(lane < 2 * N_HEADS, n2, jnp.where(lane < 3 * N_HEADS, n3, zero)))


def _rows(x2, seq, post=None, pre=None, forget=None):
    m, d = x2.shape
    bsz = m // seq
    bm = min(ROWS_TILE, seq)
    per_seq = seq // bm
    row = pl.BlockSpec((bm, d), lambda i: (i, 0))
    vec = pl.BlockSpec((1, d), lambda i: (0, 0))
    per_batch = pl.BlockSpec((1, 1, d), lambda i: (i // per_seq, 0, 0))
    args, in_specs, out_specs, out_shape, scratch = [x2], [row], [], [], []
    if post is not None:
        y, g_post, gate = post
        args += [y, g_post.reshape(1, d), gate]
        in_specs += [row, vec, per_batch]
    if pre is not None:
        g_pre, scale, shift = pre
        args += [g_pre.reshape(1, d), scale, shift]
        in_specs += [vec, per_batch, per_batch]
    if forget is not None:
        wf, fb = forget
        tri = (lax.broadcasted_iota(jnp.int32, (bm, bm), 1) <= lax.broadcasted_iota(jnp.int32, (bm, bm), 0)).astype(BF16)
        args += [wf, fb, tri]
        in_specs += [pl.BlockSpec((d, LANES), lambda i: (0, 0)), pl.BlockSpec((1, LANES), lambda i: (0, 0)),
                     pl.BlockSpec((bm, bm), lambda i: (0, 0))]
    if post is not None:
        out_specs.append(row)
        out_shape.append(jax.ShapeDtypeStruct((m, d), F32))
    if pre is not None:
        out_specs.append(row)
        out_shape.append(jax.ShapeDtypeStruct((m, d), BF16))
    if forget is not None:
        out_specs.append(pl.BlockSpec((1, bm, LANES), lambda i: (i // per_seq, i % per_seq, 0)))
        out_shape.append(jax.ShapeDtypeStruct((bsz, seq, LANES), BF16))
        scratch.append(pltpu.VMEM((1, LANES), F32))
    return pl.pallas_call(
        functools.partial(_rows_kernel, has_post=post is not None, has_pre=pre is not None,
                          has_forget=forget is not None, tiles_per_seq=per_seq),
        grid=(m // bm,),
        in_specs=in_specs, out_specs=out_specs, out_shape=out_shape, scratch_shapes=scratch,
        compiler_params=_cparams(("arbitrary",), ROWS_VMEM),
        name="rows_norm",
    )(*args)


def _mm_kernel(*refs, n_a, relu2, has_side, w_t):
    a_refs, w_ref = refs[:n_a], refs[n_a]
    if has_side:
        side_in_ref, o_ref, side_out_ref = refs[n_a + 1:n_a + 4]
        side_out_ref[...] = side_in_ref[...].astype(BF16)
    else:
        o_ref = refs[n_a + 1]
    if w_t:
        assert n_a == 1
        acc = lax.dot_general(a_refs[0][...], w_ref[0], (((1,), (1,)), ((), ())), preferred_element_type=F32)
    else:
        ksz = w_ref.shape[1] // n_a
        acc = None
        for t, a_ref in enumerate(a_refs):
            part = jnp.dot(a_ref[...], w_ref[0, t * ksz:(t + 1) * ksz, :], preferred_element_type=F32)
            acc = part if acc is None else acc + part
    if relu2:
        acc = jnp.square(jnp.maximum(acc, 0.0))
    o_ref[...] = acc.astype(o_ref.dtype)


def _matmul(a_parts, w_all, layer, out_dtype, relu2=False, side=None, w_t=False):
    m = a_parts[0].shape[0]
    if w_t:
        _, n, k = w_all.shape
    else:
        _, k, n = w_all.shape
    ka = k // len(a_parts)
    bm = min(MATMUL_TILE, m)
    bn = min(MATMUL_TILE, n)
    nn = n // bn
    in_specs = [pl.BlockSpec((bm, ka), lambda i, j: (i, 0)) for _ in a_parts]
    if w_t:
        in_specs.append(pl.BlockSpec((1, bn, k), lambda i, j: (layer, j, 0)))
    else:
        in_specs.append(pl.BlockSpec((1, k, bn), lambda i, j: (layer, 0, j)))
    out_specs = pl.BlockSpec((bm, bn), lambda i, j: (i, j))
    out_shape = jax.ShapeDtypeStruct((m, n), out_dtype)
    args = list(a_parts) + [w_all]
    if side is not None:
        src, src_layer = side
        _, rows, cols = src.shape
        n_slabs = 1
        while n_slabs * 2 <= (m // bm) * nn and rows % (n_slabs * 2) == 0 and rows // (n_slabs * 2) >= 2 * SUBLANES:
            n_slabs *= 2
        slab = rows // n_slabs
        slab_idx = lambda i, j: jnp.minimum(i * nn + j, n_slabs - 1)
        in_specs.append(pl.BlockSpec((1, slab, cols), lambda i, j: (src_layer, slab_idx(i, j), 0)))
        out_specs = [out_specs, pl.BlockSpec((1, slab, cols), lambda i, j: (0, slab_idx(i, j), 0))]
        out_shape = [out_shape, jax.ShapeDtypeStruct((1, rows, cols), BF16)]
        args.append(src)
    return pl.pallas_call(
        functools.partial(_mm_kernel, n_a=len(a_parts), relu2=relu2, has_side=side is not None, w_t=w_t),
        grid=(m // bm, nn),
        in_specs=in_specs,
        out_specs=out_specs,
        out_shape=out_shape,
        compiler_params=_cparams(("arbitrary", "arbitrary"), MATMUL_VMEM),
        name="matmul_fullk",
    )(*args)


def _mm_acc_kernel(a_ref, w_ref, o_ref, acc_ref):
    @pl.when(pl.program_id(2) == 0)
    def _():
        acc_ref[...] = jnp.zeros_like(acc_ref)

    acc = acc_ref[...] + jnp.dot(a_ref[...], w_ref[0], preferred_element_type=F32)
    acc_ref[...] = acc
    o_ref[...] = acc.astype(o_ref.dtype)


def _matmul_ksplit(a, w_all, layer, out_dtype):
    m, k = a.shape
    n = w_all.shape[2]
    bm = min(MATMUL_TILE, m)
    bn = min(MATMUL_TILE, n)
    bk = min(MATMUL_K_TILE, k)
    return pl.pallas_call(
        _mm_acc_kernel,
        grid=(m // bm, n // bn, k // bk),
        in_specs=[
            pl.BlockSpec((bm, bk), lambda i, j, q: (i, q)),
            pl.BlockSpec((1, bk, bn), lambda i, j, q: (layer, q, j)),
        ],
        out_specs=pl.BlockSpec((bm, bn), lambda i, j, q: (i, j)),
        out_shape=jax.ShapeDtypeStruct((m, n), out_dtype),
        scratch_shapes=[pltpu.VMEM((bm, bn), F32)],
        compiler_params=_cparams(("arbitrary", "arbitrary", "arbitrary"), MATMUL_VMEM),
        name="matmul_ksplit",
    )(a, w_all)


def _layer_norm_rows(h, g, b):
    mu = jnp.mean(h, axis=-1, keepdims=True)
    hc = h - mu
    var = jnp.mean(hc * hc, axis=-1, keepdims=True)
    return hc * lax.rsqrt(var + EPS) * g + b


def _conv_kernel(a_ref, b_ref, ah_ref, bh_ref, w_ref, cb_ref, lg_ref, lb_ref, o_ref, buf_ref, acc_ref, *, bs):
    i = pl.program_id(1)
    glu = a_ref[0].astype(F32) * jax.nn.sigmoid(b_ref[0].astype(F32))
    halo = ah_ref[0].astype(F32) * jax.nn.sigmoid(bh_ref[0].astype(F32))
    halo = jnp.where(i == 0, 0.0, halo)
    n_cb = D_GROUP // LANES
    for cb in range(n_cb):
        cs = slice(cb * LANES, (cb + 1) * LANES)
        buf_ref[cb, 0:CONV_HALO, :] = halo[:, cs]
        buf_ref[cb, CONV_HALO:CONV_HALO + bs, :] = glu[:, cs]
    first = CONV_HALO - (CONV_WIDTH - 1)
    n_u = bs // SUBLANES

    def per_col_block(cb, carry):
        taps = [jnp.broadcast_to(w_ref[cb, j:j + 1, :], (SUBLANES, LANES)) for j in range(CONV_WIDTH)]
        acc = [jnp.broadcast_to(cb_ref[cb], (SUBLANES, LANES)) for _ in range(n_u)]
        for r in range(SUBLANES):
            js = [j for j in range(CONV_WIDTH) if (first + j) % SUBLANES == r]
            ks = [(first + j) // SUBLANES for j in js]
            for v in range(min(ks), n_u + max(ks)):
                window = buf_ref[cb, r + SUBLANES * v:r + SUBLANES * (v + 1), :]
                for j, k in zip(js, ks):
                    if 0 <= v - k < n_u:
                        acc[v - k] = acc[v - k] + taps[j] * window
        for u in range(n_u):
            acc_ref[cb, u * SUBLANES:(u + 1) * SUBLANES, :] = acc[u]
        return carry

    lax.fori_loop(0, n_cb, per_col_block, 0)
    h = jnp.concatenate([acc_ref[cb] for cb in range(n_cb)], axis=1)
    h = _layer_norm_rows(h, lg_ref[...], lb_ref[...])
    o_ref[0] = (h * jax.nn.sigmoid(h)).astype(BF16)


def _conv_module(z_lo, conv_w, conv_b, ln_g, ln_b):
    bsz, seq, _ = z_lo.shape
    bs = min(CONV_TILE, seq)
    hb = bs // CONV_HALO
    vec = lambda v: v.reshape(1, D_GROUP)
    tile = lambda col: pl.BlockSpec((1, bs, D_GROUP), lambda b, i: (b, i, col))
    halo = lambda col: pl.BlockSpec((1, CONV_HALO, D_GROUP), lambda b, i: (b, jnp.maximum(i * hb - 1, 0), col))
    full = lambda r: pl.BlockSpec((r, D_GROUP), lambda b, i: (0, 0))
    n_cb = D_GROUP // LANES
    per_cb = lambda rows: pl.BlockSpec((n_cb, rows, LANES), lambda b, i: (0, 0, 0))
    w_cb = conv_w.reshape(CONV_WIDTH, n_cb, LANES).transpose(1, 0, 2)
    return pl.pallas_call(
        functools.partial(_conv_kernel, bs=bs),
        grid=(bsz, seq // bs),
        in_specs=[tile(ZLO_GLU_A // D_GROUP), tile(ZLO_GLU_B // D_GROUP),
                  halo(ZLO_GLU_A // D_GROUP), halo(ZLO_GLU_B // D_GROUP),
                  per_cb(CONV_WIDTH), per_cb(1), full(1), full(1)],
        out_specs=pl.BlockSpec((1, bs, D_GROUP), lambda b, i: (b, i, 0)),
        out_shape=jax.ShapeDtypeStruct((bsz, seq, D_GROUP), BF16),
        scratch_shapes=[pltpu.VMEM((n_cb, CONV_HALO + bs, LANES), F32),
                        pltpu.VMEM((n_cb, bs, LANES), F32)],
        compiler_params=_cparams(("arbitrary", "arbitrary"), MIXER_VMEM),
        name="conv_module",
    )(z_lo, z_lo, z_lo, z_lo, w_cb, conv_b.reshape(n_cb, 1, LANES), vec(ln_g), vec(ln_b))


def _fox_kernel(q_ref, k_ref, v_ref, c3_ref, o_ref, ka_ref, va_ref, s0_ref, s1_ref, p_ref, m_ref, acc_ref,
                *, blk, sub, rg):
    h = pl.program_id(1)
    i = pl.program_id(2)

    @pl.when(i == 0)
    def _():
        ka_ref[:, :HEAD_DIM] = k_ref[0]
        ka_ref[:, HEAD_DIM:] = c3_ref[0]
        va_ref[:, :HEAD_DIM] = v_ref[0]
        lane = lax.broadcasted_iota(jnp.int32, (va_ref.shape[0], HEAD_DIM), 1)
        va_ref[:, HEAD_DIM:] = (lane == 0).astype(BF16)

    lane = lax.broadcasted_iota(jnp.int32, (blk, HEAD_DIM), 1)
    pick = (lane == h) | (lane == h + N_HEADS) | (lane == h + 2 * N_HEADS)
    qa = jnp.concatenate([q_ref[0], pick.astype(BF16)], axis=1)
    m_ref[...] = jnp.full(m_ref.shape, -jnp.inf, F32)
    acc_ref[...] = jnp.zeros(acc_ref.shape, F32)

    def logits_to(s_ref, jsub, r0=0):
        start = pl.multiple_of(jsub * sub, sub)
        s_ref[r0:, :] = lax.dot_general(qa[r0:], ka_ref[pl.ds(start, sub), :], (((1,), (1,)), ((), ())),
                                        preferred_element_type=F32)

    def update(s_ref, jsub, mask_off=None, r0=0):
        for r in range(r0 // rg, blk // rg):
            rows = slice(r * rg, (r + 1) * rg)
            s = s_ref[rows, :]
            if mask_off is not None and r * rg < mask_off + sub - 1:
                row = lax.broadcasted_iota(jnp.int32, (rg, sub), 0) + r * rg
                col = lax.broadcasted_iota(jnp.int32, (rg, sub), 1) + mask_off
                s = jnp.where(col <= row, s, -jnp.inf)
            m_old = m_ref[rows, :]
            m_new = jnp.maximum(m_old, jnp.max(s, axis=-1, keepdims=True))
            m_ref[rows, :] = m_new
            alpha = jnp.exp2(m_old - m_new)
            for cg in range(sub // LANES):
                cs = slice(cg * LANES, (cg + 1) * LANES)
                p_ref[rows, cs] = jnp.exp2(s[:, cs] - m_new).astype(BF16)
            for cg in range(acc_ref.shape[1] // LANES):
                cs = slice(cg * LANES, (cg + 1) * LANES)
                acc_ref[rows, cs] = alpha * acc_ref[rows, cs]
        start = pl.multiple_of(jsub * sub, sub)
        acc_ref[r0:, :] += jnp.dot(p_ref[r0:, :], va_ref[pl.ds(start, sub), :], preferred_element_type=F32)

    per = blk // sub
    logits_to(s0_ref, 0)

    def body(t, carry):
        logits_to(s1_ref, per * t + 1)
        update(s0_ref, per * t)
        logits_to(s0_ref, per * t + 2)
        update(s1_ref, per * t + 1)
        return carry

    lax.fori_loop(0, i, body, 0)
    logits_to(s1_ref, per * i + 1, r0=sub)
    update(s0_ref, per * i, mask_off=0)
    update(s1_ref, per * i + 1, mask_off=sub, r0=sub)
    acc = acc_ref[...]
    o_ref[0] = (acc[:, :HEAD_DIM] * (1.0 / acc[:, HEAD_DIM:HEAD_DIM + 1])).astype(BF16)


def _forgetting_attention(z_lo, c3):
    bsz, seq, _ = z_lo.shape
    blk = min(FOX_BLOCK, seq)
    sub = blk // 2
    rg = min(FOX_ROW_GROUP, blk)
    col = lambda off: off // HEAD_DIM
    kv = lambda off: pl.BlockSpec((1, seq, HEAD_DIM), lambda b, h, i: (b, 0, col(off) + h))
    return pl.pallas_call(
        functools.partial(_fox_kernel, blk=blk, sub=sub, rg=rg),
        grid=(bsz, N_HEADS, seq // blk),
        in_specs=[pl.BlockSpec((1, blk, HEAD_DIM), lambda b, h, i: (b, i, col(ZLO_FQ) + h)),
                  kv(ZLO_FK), kv(ZLO_FV),
                  pl.BlockSpec((1, seq, LANES), lambda b, h, i: (b, 0, 0))],
        out_specs=pl.BlockSpec((1, blk, HEAD_DIM), lambda b, h, i: (b, i, h)),
        out_shape=jax.ShapeDtypeStruct((bsz, seq, D_GROUP), BF16),
        scratch_shapes=[pltpu.VMEM((seq, 2 * HEAD_DIM), BF16), pltpu.VMEM((seq, 2 * HEAD_DIM), BF16),
                        pltpu.VMEM((blk, sub), F32), pltpu.VMEM((blk, sub), F32), pltpu.VMEM((blk, sub), BF16),
                        pltpu.VMEM((blk, LANES), F32),
                        pltpu.VMEM((blk, 2 * HEAD_DIM), F32)],
        compiler_params=_cparams(("arbitrary", "arbitrary", "arbitrary"), FOX_VMEM),
        name="forgetting_attention",
    )(z_lo, z_lo, z_lo, c3)


def _ret_kernel(q_ref, k_ref, v_ref, g_ref, cos_ref, sin_ref, dec_ref, qw_ref, kw_ref, cd_ref, bd_ref,
                o_ref, state_ref, *, chunk):
    n = pl.program_id(1)

    @pl.when(n == 0)
    def _():
        state_ref[...] = jnp.zeros_like(state_ref)

    cos = cos_ref[...]
    sin = sin_ref[...]
    lane = lax.broadcasted_iota(jnp.int32, (chunk, LANES), 1)
    first_half = (lane & (RET_K_DIM // 2)) == 0
    low_head = lane < RET_K_DIM

    def rotate(x):
        swapped = jnp.where(first_half, pltpu.roll(x, LANES - RET_K_DIM // 2, 1), pltpu.roll(x, RET_K_DIM // 2, 1))
        return x * cos + swapped * sin

    for pr in range(N_HEADS // 2):
        ls = slice(pr * LANES, (pr + 1) * LANES)
        qr = rotate(q_ref[0, :, ls].astype(F32))
        kr = rotate(k_ref[0, :, ls].astype(F32))
        kb = kr.astype(BF16)
        v_pair = v_ref[0, :, pr * 2 * HEAD_DIM:(pr + 1) * 2 * HEAD_DIM]
        state = state_ref[pr]
        cross = jnp.dot((qr * qw_ref[:, ls]).astype(BF16), state.astype(BF16), preferred_element_type=F32)
        for a in range(2):
            head = 2 * pr + a
            hs = slice(head * HEAD_DIM, (head + 1) * HEAD_DIM)
            qa = jnp.where(low_head if a == 0 else jnp.logical_not(low_head), qr, 0.0).astype(BF16)
            scores = lax.dot_general(qa, kb, (((1,), (1,)), ((), ())), preferred_element_type=F32) * dec_ref[head]
            inner = jnp.dot(scores.astype(BF16), v_pair[:, a * HEAD_DIM:(a + 1) * HEAD_DIM],
                            preferred_element_type=F32)
            y = inner + cross[:, a * HEAD_DIM:(a + 1) * HEAD_DIM]
            mu = jnp.mean(y, axis=-1, keepdims=True)
            yc = y - mu
            var = jnp.mean(yc * yc, axis=-1, keepdims=True)
            yn = yc * lax.rsqrt(var + GN_EPS)
            gate = g_ref[0, :, hs].astype(F32)
            o_ref[0, :, hs] = (gate * jax.nn.sigmoid(gate) * yn).astype(BF16)
        kw_t = (kr * kw_ref[:, ls]).T.astype(BF16)
        kv = jnp.dot(kw_t, v_pair, preferred_element_type=F32)
        state_ref[pr] = state * cd_ref[pr] + kv * bd_ref[...]


def _retention_tables(seq, chunk):
    half = RET_K_DIM // 2
    inv = 1.0 / (ROPE_BASE ** jnp.linspace(0.0, 1.0, half, dtype=F32))
    ang = jnp.arange(seq).astype(F32)[:, None] * inv[None, :]
    cos = jnp.tile(jnp.cos(ang), (1, LANES // half))
    sin = jnp.tile(jnp.concatenate([-jnp.sin(ang), jnp.sin(ang)], axis=-1), (1, LANES // RET_K_DIM))
    log_gamma = jnp.log(1.0 - 2.0 ** (-5.0 - jnp.arange(N_HEADS, dtype=F32)))
    idx = jnp.arange(chunk, dtype=F32)
    rel = idx[:, None] - idx[None, :]
    decay = jnp.where(rel >= 0, jnp.exp(log_gamma[:, None, None] * jnp.maximum(rel, 0.0)), 0.0)
    k_w = jnp.exp(log_gamma[:, None] * (chunk - 1.0 - idx)[None, :])
    q_w = jnp.exp(log_gamma[:, None] * (idx + 1.0)[None, :])
    per_lane = lambda t: jnp.repeat(t.T, RET_K_DIM, axis=1)
    chunk_decay = jnp.exp(log_gamma * chunk)
    cd = jnp.broadcast_to(jnp.repeat(chunk_decay, RET_K_DIM).reshape(N_HEADS // 2, LANES, 1),
                          (N_HEADS // 2, LANES, 2 * HEAD_DIM))
    r = jnp.arange(LANES)[:, None] // RET_K_DIM
    c = jnp.arange(2 * HEAD_DIM)[None, :] // HEAD_DIM
    bd = (r == c).astype(F32)
    k_scale = RET_K_DIM ** -0.5
    return cos, sin, decay * k_scale, per_lane(q_w), per_lane(k_w) * k_scale, cd, bd


def _retention(z_hi):
    bsz, seq, _ = z_hi.shape
    chunk = min(RET_CHUNK, seq)
    cos, sin, decay, q_w, k_w, cd, bd = _retention_tables(seq, chunk)
    d_qk = N_HEADS * RET_K_DIM
    const = lambda shape: pl.BlockSpec(shape, lambda b, n: (0,) * len(shape))
    return pl.pallas_call(
        functools.partial(_ret_kernel, chunk=chunk),
        grid=(bsz, seq // chunk),
        in_specs=[pl.BlockSpec((1, chunk, d_qk), lambda b, n: (b, n, ZHI_RQ // d_qk)),
                  pl.BlockSpec((1, chunk, d_qk), lambda b, n: (b, n, ZHI_RK // d_qk)),
                  pl.BlockSpec((1, chunk, D_GROUP), lambda b, n: (b, n, ZHI_RV // D_GROUP)),
                  pl.BlockSpec((1, chunk, D_GROUP), lambda b, n: (b, n, ZHI_RG // D_GROUP)),
                  pl.BlockSpec((chunk, LANES), lambda b, n: (n, 0)),
                  pl.BlockSpec((chunk, LANES), lambda b, n: (n, 0)),
                  const((N_HEADS, chunk, chunk)), const((chunk, d_qk)), const((chunk, d_qk)),
                  const((N_HEADS // 2, LANES, 2 * HEAD_DIM)), const((LANES, 2 * HEAD_DIM))],
        out_specs=pl.BlockSpec((1, chunk, D_GROUP), lambda b, n: (b, n, 0)),
        out_shape=jax.ShapeDtypeStruct((bsz, seq, D_GROUP), BF16),
        scratch_shapes=[pltpu.VMEM((N_HEADS // 2, LANES, 2 * HEAD_DIM), F32)],
        compiler_params=_cparams(("arbitrary", "arbitrary"), MIXER_VMEM),
        name="retention",
    )(z_hi, z_hi, z_hi, z_hi, cos, sin, decay, q_w, k_w, cd, bd)


def _gelu_tanh(x):
    return 0.5 * x * (1.0 + jnp.tanh(math.sqrt(2.0 / math.pi) * (x + 0.044715 * (x * x * x))))


def _sg_kernel(u_ref, v_ref, w_ref, bt_ref, lg_ref, lb_ref, o_ref, *, bs):
    u = _gelu_tanh(u_ref[0].astype(F32))
    v = _layer_norm_rows(_gelu_tanh(v_ref[0].astype(F32)), lg_ref[...], lb_ref[...]).astype(BF16)
    row = lax.broadcasted_iota(jnp.int32, (SG_CHUNK, SG_CHUNK), 0)
    col = lax.broadcasted_iota(jnp.int32, (SG_CHUNK, SG_CHUNK), 1)
    causal = col <= row
    for g in range(N_HEADS):
        cs = slice(g * HEAD_DIM, (g + 1) * HEAD_DIM)
        w_g = jnp.where(causal, w_ref[g], 0.0).astype(BF16)
        bias = bt_ref[:, g:g + 1]
        for r in range(bs // SG_CHUNK):
            rs = slice(r * SG_CHUNK, (r + 1) * SG_CHUNK)
            mixed = jnp.dot(w_g, v[rs, cs], preferred_element_type=F32) + bias
            o_ref[0, rs, cs] = (u[rs, cs] * mixed).astype(BF16)


def _spatial_gating(z_hi, sg_w, sg_b, ln_g, ln_b):
    bsz, seq, _ = z_hi.shape
    bs = min(SG_TILE, seq)
    vec = lambda t: t.reshape(1, D_GROUP)
    tile = lambda off: pl.BlockSpec((1, bs, D_GROUP), lambda b, i: (b, i, off // D_GROUP))
    return pl.pallas_call(
        functools.partial(_sg_kernel, bs=bs),
        grid=(bsz, seq // bs),
        in_specs=[tile(ZHI_SGU), tile(ZHI_SGV),
                  pl.BlockSpec((N_HEADS, SG_CHUNK, SG_CHUNK), lambda b, i: (0, 0, 0)),
                  pl.BlockSpec((SG_CHUNK, N_HEADS), lambda b, i: (0, 0)),
                  pl.BlockSpec((1, D_GROUP), lambda b, i: (0, 0)),
                  pl.BlockSpec((1, D_GROUP), lambda b, i: (0, 0))],
        out_specs=pl.BlockSpec((1, bs, D_GROUP), lambda b, i: (b, i, 0)),
        out_shape=jax.ShapeDtypeStruct((bsz, seq, D_GROUP), BF16),
        compiler_params=_cparams(("arbitrary", "arbitrary"), MIXER_VMEM),
        name="spatial_gating",
    )(z_hi, z_hi, sg_w, sg_b.T, vec(ln_g), vec(ln_b))


def _forget_weights(w_gate_l, f_bias_l):
    d = w_gate_l.shape[0]
    wf = jnp.concatenate([w_gate_l, w_gate_l, w_gate_l, jnp.zeros((d, LANES - 3 * N_HEADS), F32)], axis=1).astype(BF16)
    fb = jnp.concatenate([f_bias_l, f_bias_l, f_bias_l, jnp.zeros((LANES - 3 * N_HEADS,), F32)]).reshape(1, LANES)
    return wf, fb


def kernel(x, c, ada_w, ada_b, mix_pre_g, mix_post_g, w_in, fox_f_bias, conv_w, conv_b, conv_ln_g, conv_ln_b,
           sg_w, sg_b, sg_ln_g, sg_ln_b, w_out, ffn_pre_g, ffn_post_g, w_ff1, w_ff2):
    bsz, seq, d = x.shape
    n_layers = ada_w.shape[0]
    m = bsz * seq
    mod = _modulation(c, ada_w, ada_b).reshape(n_layers, bsz, 6, 1, d)
    shift1, scale1, gate1, shift2, scale2, gate2 = [mod[:, :, t] for t in range(6)]

    w_in_t = jnp.transpose(w_in, (0, 2, 1))
    q_scale = jnp.ones((FF_OFF, 1), F32).at[ZLO_FQ:ZLO_FQ + D_GROUP].set(FOX_SCALE)
    w_lo_b = (w_in_t[:, :FF_OFF, :] * q_scale).astype(BF16)
    w_hi_b = w_in_t[:, FF_OFF + N_HEADS:, :].astype(BF16)
    w_gate = w_in[:, :, FF_OFF:FF_OFF + N_HEADS]

    x2 = x.reshape(m, d)
    h, c3 = _rows(x2, seq, pre=(mix_pre_g[0], scale1[0], shift1[0]),
                  forget=_forget_weights(w_gate[0], fox_f_bias[0]))
    for l in range(n_layers):
        z_lo, w_out_b = _matmul([h], w_lo_b, l, BF16, side=(w_out, l), w_t=True)
        z_hi, w_ff1_b = _matmul([h], w_hi_b, l, BF16, side=(w_ff1, l), w_t=True)
        z_lo = z_lo.reshape(bsz, seq, D_ZHALF)
        z_hi = z_hi.reshape(bsz, seq, D_ZHALF)
        y_conv = _conv_module(z_lo, conv_w[l], conv_b[l], conv_ln_g[l], conv_ln_b[l])
        y_fox = _forgetting_attention(z_lo, c3)
        y_ret = _retention(z_hi)
        y_sg = _spatial_gating(z_hi, sg_w[l], sg_b[l], sg_ln_g[l], sg_ln_b[l])
        parts = [t.reshape(m, D_GROUP) for t in (y_conv, y_fox, y_ret, y_sg)]
        y = _matmul(parts, w_out_b, 0, BF16)
        x2, h = _rows(x2, seq, post=(y, mix_post_g[l], gate1[l]), pre=(ffn_pre_g[l], scale2[l], shift2[l]))
        u, w_ff2_b = _matmul([h], w_ff1_b, 0, BF16, relu2=True, side=(w_ff2, l))
        y = _matmul_ksplit(u, w_ff2_b, 0, BF16)
        if l + 1 < n_layers:
            x2, h, c3 = _rows(x2, seq, post=(y, ffn_post_g[l], gate2[l]),
                              pre=(mix_pre_g[l + 1], scale1[l + 1], shift1[l + 1]),
                              forget=_forget_weights(w_gate[l + 1], fox_f_bias[l + 1]))
        else:
            (x2,) = _rows(x2, seq, post=(y, ffn_post_g[l], gate2[l]))
    return x2.reshape(bsz, seq, d)
```

```python
import functools
import math

import jax
import jax.numpy as jnp
from jax import lax
from jax.experimental import pallas as pl
from jax.experimental.pallas import tpu as pltpu

F32 = jnp.float32
BF16 = jnp.bfloat16

D_GROUP = 1024
N_HEADS = 8
HEAD_DIM = D_GROUP // N_HEADS
CONV_WIDTH = 31
CONV_HALO = 32
RET_K_DIM = HEAD_DIM // 2
SG_CHUNK = 128
ROPE_BASE = 10000.0
EPS = 1e-6
GN_EPS = 1e-5
LOG2E = math.log2(math.e)
FOX_SCALE = LOG2E * HEAD_DIM ** -0.5

D_ZHALF = 5120
FF_OFF = 5120
ZLO_GLU_A, ZLO_GLU_B, ZLO_FQ, ZLO_FK, ZLO_FV = 0, 1024, 2048, 3072, 4096
ZHI_RQ, ZHI_RK, ZHI_RV, ZHI_RG, ZHI_SGU, ZHI_SGV = 0, 512, 1024, 2048, 3072, 4096

MATMUL_TILE = 1024
MATMUL_K_TILE = 4096
MATMUL_VMEM = 56
ROWS_TILE = 256
ROWS_VMEM = 48
MOD_SLAB = 128
MOD_VMEM = 48
CONV_TILE = 128
FOX_BLOCK = 2048
FOX_ROW_GROUP = 128
FOX_VMEM = 56
RET_CHUNK = 256
SG_TILE = 512
MIXER_VMEM = 32

MIB = 1024 * 1024
LANES = 128
SUBLANES = 8


def _cparams(semantics, vmem_mib):
    return pltpu.CompilerParams(dimension_semantics=semantics, vmem_limit_bytes=vmem_mib * MIB)


def _split3(x):
    p1 = x.astype(BF16)
    r1 = x - p1.astype(F32)
    p2 = r1.astype(BF16)
    p3 = (r1 - p2.astype(F32)).astype(BF16)
    return p1, p2, p3


def _mod_kernel(ct_ref, w_ref, b_ref, o_ref, sb_ref, acc_ref, *, bsz):
    q = pl.program_id(1)

    @pl.when((pl.program_id(0) == 0) & (q == 0))
    def _():
        for b in range(bsz):
            cb = ct_ref[:, b:b + 1]
            sb_ref[b] = jnp.broadcast_to(cb * jax.nn.sigmoid(cb), sb_ref.shape[1:])

    @pl.when(q == 0)
    def _():
        acc_ref[...] = jnp.zeros_like(acc_ref)

    kc, n = w_ref.shape[1], w_ref.shape[2]
    start = pl.multiple_of(q * kc, kc)
    s_rows = [sb_ref[b, pl.ds(start, kc), :] for b in range(bsz)]
    for cg in range(n // LANES):
        cs = slice(cg * LANES, (cg + 1) * LANES)
        w = w_ref[0, :, cs]
        for b in range(bsz):
            acc_ref[b, :, cs] += jnp.sum((w * s_rows[b]).reshape(kc // SUBLANES, SUBLANES, LANES), axis=0)

    @pl.when(q == pl.num_programs(1) - 1)
    def _():
        o_ref[0] = jnp.zeros(o_ref.shape[1:], F32)
        for b in range(bsz):
            o_ref[0, b:b + 1, :] = jnp.sum(acc_ref[b], axis=0, keepdims=True) + b_ref[0]


def _modulation(c, ada_w, ada_b):
    n_layers, d, n = ada_w.shape
    bsz = c.shape[0]
    rows = SUBLANES
    kc = MOD_SLAB
    out = pl.pallas_call(
        functools.partial(_mod_kernel, bsz=bsz),
        grid=(n_layers, d // kc),
        in_specs=[
            pl.BlockSpec((d, bsz), lambda l, q: (0, 0)),
            pl.BlockSpec((1, kc, n), lambda l, q: (l, q, 0)),
            pl.BlockSpec((1, 1, n), lambda l, q: (l, 0, 0)),
        ],
        out_specs=pl.BlockSpec((1, rows, n), lambda l, q: (l, 0, 0)),
        out_shape=jax.ShapeDtypeStruct((n_layers, rows, n), F32),
        scratch_shapes=[pltpu.VMEM((bsz, d, LANES), F32), pltpu.VMEM((bsz, SUBLANES, n), F32)],
        compiler_params=_cparams(("arbitrary", "arbitrary"), MOD_VMEM),
        name="adaln_mod",
    )(c.T, ada_w, ada_b.reshape(n_layers, 1, n))
    return out[:, :bsz]


def _log_sigmoid(x):
    return jnp.minimum(x, 0.0) - jnp.log(1.0 + jnp.exp(-jnp.abs(x)))


def _rows_kernel(*refs, has_post, has_pre, has_forget, tiles_per_seq):
    it = iter(refs)
    x_ref = next(it)
    if has_post:
        y_ref, gpost_ref, gate_ref = next(it), next(it), next(it)
    if has_pre:
        gpre_ref, sc_ref, sh_ref = next(it), next(it), next(it)
    if has_forget:
        wf_ref, fb_ref, tri_ref = next(it), next(it), next(it)
    if has_post:
        xo_ref = next(it)
    if has_pre:
        h_ref = next(it)
    if has_forget:
        c3_ref, carry_ref = next(it), next(it)

    x = x_ref[...]
    if has_post:
        y = y_ref[...].astype(F32)
        r = lax.rsqrt(jnp.mean(y * y, axis=-1, keepdims=True) + EPS)
        x = x + gate_ref[0] * (y * r * gpost_ref[...])
        xo_ref[...] = x
    if not has_pre:
        return
    r = lax.rsqrt(jnp.mean(x * x, axis=-1, keepdims=True) + EPS)
    hb = (x * r * (gpre_ref[...] * (1.0 + sc_ref[0])) + sh_ref[0]).astype(BF16)
    h_ref[...] = hb
    if not has_forget:
        return
    logf = _log_sigmoid(jnp.dot(hb, wf_ref[...], preferred_element_type=F32) + fb_ref[...])
    tri = tri_ref[...]
    local = sum(jnp.dot(tri, p, preferred_element_type=F32) for p in _split3(logf))

    @pl.when(pl.program_id(0) % tiles_per_seq == 0)
    def _():
        carry_ref[...] = jnp.zeros_like(carry_ref)

    cum = carry_ref[...] + local
    bm = cum.shape[0]
    carry_ref[...] = cum[bm - 1:bm, :]
    n1, n2, n3 = _split3(cum * (-LOG2E))
    lane = lax.broadcasted_iota(jnp.int32, cum.shape, 1)
    zero = jnp.zeros_like(n1)
    c3_ref[0] = jnp.where(lane < N_HEADS, n1,
                          jnp.where(lane < 2 * N_HEADS, n2, jnp.where(lane < 3 * N_HEADS, n3, zero)))


def _rows(x2, seq, post=None, pre=None, forget=None):
    m, d = x2.shape
    bsz = m // seq
    bm = min(ROWS_TILE, seq)
    per_seq = seq // bm
    row = pl.BlockSpec((bm, d), lambda i: (i, 0))
    vec = pl.BlockSpec((1, d), lambda i: (0, 0))
    per_batch = pl.BlockSpec((1, 1, d), lambda i: (i // per_seq, 0, 0))
    args, in_specs, out_specs, out_shape, scratch = [x2], [row], [], [], []
    if post is not None:
        y, g_post, gate = post
        args += [y, g_post.reshape(1, d), gate]
        in_specs += [row, vec, per_batch]
    if pre is not None:
        g_pre, scale, shift = pre
        args += [g_pre.reshape(1, d), scale, shift]
        in_specs += [vec, per_batch, per_batch]
    if forget is not None:
        wf, fb = forget
        tri = (lax.broadcasted_iota(jnp.int32, (bm, bm), 1) <= lax.broadcasted_iota(jnp.int32, (bm, bm), 0)).astype(BF16)
        args += [wf, fb, tri]
        in_specs += [pl.BlockSpec((d, LANES), lambda i: (0, 0)), pl.BlockSpec((1, LANES), lambda i: (0, 0)),
                     pl.BlockSpec((bm, bm), lambda i: (0, 0))]
    if post is not None:
        out_specs.append(row)
        out_shape.append(jax.ShapeDtypeStruct((m, d), F32))
    if pre is not None:
        out_specs.append(row)
        out_shape.append(jax.ShapeDtypeStruct((m, d), BF16))
    if forget is not None:
        out_specs.append(pl.BlockSpec((1, bm, LANES), lambda i: (i // per_seq, i % per_seq, 0)))
        out_shape.append(jax.ShapeDtypeStruct((bsz, seq, LANES), BF16))
        scratch.append(pltpu.VMEM((1, LANES), F32))
    return pl.pallas_call(
        functools.partial(_rows_kernel, has_post=post is not None, has_pre=pre is not None,
                          has_forget=forget is not None, tiles_per_seq=per_seq),
        grid=(m // bm,),
        in_specs=in_specs, out_specs=out_specs, out_shape=out_shape, scratch_shapes=scratch,
        compiler_params=_cparams(("arbitrary",), ROWS_VMEM),
        name="rows_norm",
    )(*args)


def _mm_kernel(*refs, n_a, relu2, has_side, w_t):
    a_refs, w_ref = refs[:n_a], refs[n_a]
    if has_side:
        side_in_ref, o_ref, side_out_ref = refs[n_a + 1:n_a + 4]
        side_out_ref[...] = side_in_ref[...].astype(BF16)
    else:
        o_ref = refs[n_a + 1]
    if w_t:
        assert n_a == 1
        acc = lax.dot_general(a_refs[0][...], w_ref[0], (((1,), (1,)), ((), ())), preferred_element_type=F32)
    else:
        ksz = w_ref.shape[1] // n_a
        acc = None
        for t, a_ref in enumerate(a_refs):
            part = jnp.dot(a_ref[...], w_ref[0, t * ksz:(t + 1) * ksz, :], preferred_element_type=F32)
            acc = part if acc is None else acc + part
    if relu2:
        acc = jnp.square(jnp.maximum(acc, 0.0))
    o_ref[...] = acc.astype(o_ref.dtype)


def _matmul(a_parts, w_all, layer, out_dtype, relu2=False, side=None, w_t=False):
    m = a_parts[0].shape[0]
    if w_t:
        _, n, k = w_all.shape
    else:
        _, k, n = w_all.shape
    ka = k // len(a_parts)
    bm = min(MATMUL_TILE, m)
    bn = min(MATMUL_TILE, n)
    nn = n // bn
    in_specs = [pl.BlockSpec((bm, ka), lambda i, j: (i, 0)) for _ in a_parts]
    if w_t:
        in_specs.append(pl.BlockSpec((1, bn, k), lambda i, j: (layer, j, 0)))
    else:
        in_specs.append(pl.BlockSpec((1, k, bn), lambda i, j: (layer, 0, j)))
    out_specs = pl.BlockSpec((bm, bn), lambda i, j: (i, j))
    out_shape = jax.ShapeDtypeStruct((m, n), out_dtype)
    args = list(a_parts) + [w_all]
    if side is not None:
        src, src_layer = side
        _, rows, cols = src.shape
        n_slabs = 1
        while n_slabs * 2 <= (m // bm) * nn and rows % (n_slabs * 2) == 0 and rows // (n_slabs * 2) >= 2 * SUBLANES:
            n_slabs *= 2
        slab = rows // n_slabs
        slab_idx = lambda i, j: jnp.minimum(i * nn + j, n_slabs - 1)
        in_specs.append(pl.BlockSpec((1, slab, cols), lambda i, j: (src_layer, slab_idx(i, j), 0)))
        out_specs = [out_specs, pl.BlockSpec((1, slab, cols), lambda i, j: (0, slab_idx(i, j), 0))]
        out_shape = [out_shape, jax.ShapeDtypeStruct((1, rows, cols), BF16)]
        args.append(src)
    return pl.pallas_call(
        functools.partial(_mm_kernel, n_a=len(a_parts), relu2=relu2, has_side=side is not None, w_t=w_t),
        grid=(m // bm, nn),
        in_specs=in_specs,
        out_specs=out_specs,
        out_shape=out_shape,
        compiler_params=_cparams(("arbitrary", "arbitrary"), MATMUL_VMEM),
        name="matmul_fullk",
    )(*args)


def _mm_acc_kernel(a_ref, w_ref, o_ref, acc_ref):
    @pl.when(pl.program_id(2) == 0)
    def _():
        acc_ref[...] = jnp.zeros_like(acc_ref)

    acc = acc_ref[...] + jnp.dot(a_ref[...], w_ref[0], preferred_element_type=F32)
    acc_ref[...] = acc
    o_ref[...] = acc.astype(o_ref.dtype)


def _matmul_ksplit(a, w_all, layer, out_dtype):
    m, k = a.shape
    n = w_all.shape[2]
    bm = min(MATMUL_TILE, m)
    bn = min(MATMUL_TILE, n)
    bk = min(MATMUL_K_TILE, k)
    return pl.pallas_call(
        _mm_acc_kernel,
        grid=(m // bm, n // bn, k // bk),
        in_specs=[
            pl.BlockSpec((bm, bk), lambda i, j, q: (i, q)),
            pl.BlockSpec((1, bk, bn), lambda i, j, q: (layer, q, j)),
        ],
        out_specs=pl.BlockSpec((bm, bn), lambda i, j, q: (i, j)),
        out_shape=jax.ShapeDtypeStruct((m, n), out_dtype),
        scratch_shapes=[pltpu.VMEM((bm, bn), F32)],
        compiler_params=_cparams(("arbitrary", "arbitrary", "arbitrary"), MATMUL_VMEM),
        name="matmul_ksplit",
    )(a, w_all)


def _layer_norm_rows(h, g, b):
    mu = jnp.mean(h, axis=-1, keepdims=True)
    hc = h - mu
    var = jnp.mean(hc * hc, axis=-1, keepdims=True)
    return hc * lax.rsqrt(var + EPS) * g + b


def _conv_kernel(a_ref, b_ref, w_ref, cb_ref, lg_ref, lb_ref, o_ref, buf_ref, acc_ref, *, bs):
    i = pl.program_id(1)
    glu = a_ref[0].astype(F32) * jax.nn.sigmoid(b_ref[0].astype(F32))
    n_cb = D_GROUP // LANES

    @pl.when(i == 0)
    def _():
        buf_ref[:, 0:CONV_HALO, :] = jnp.zeros((n_cb, CONV_HALO, LANES), F32)

    @pl.when(i > 0)
    def _():
        buf_ref[:, 0:CONV_HALO, :] = buf_ref[:, bs:bs + CONV_HALO, :]

    for cb in range(n_cb):
        cs = slice(cb * LANES, (cb + 1) * LANES)
        buf_ref[cb, CONV_HALO:CONV_HALO + bs, :] = glu[:, cs]
    first = CONV_HALO - (CONV_WIDTH - 1)
    n_u = bs // SUBLANES

    def per_col_block(cb, carry):
        taps = [jnp.broadcast_to(w_ref[cb, j:j + 1, :], (SUBLANES, LANES)) for j in range(CONV_WIDTH)]
        acc = [jnp.broadcast_to(cb_ref[cb], (SUBLANES, LANES)) for _ in range(n_u)]
        for r in range(SUBLANES):
            js = [j for j in range(CONV_WIDTH) if (first + j) % SUBLANES == r]
            ks = [(first + j) // SUBLANES for j in js]
            for v in range(min(ks), n_u + max(ks)):
                window = buf_ref[cb, r + SUBLANES * v:r + SUBLANES * (v + 1), :]
                for j, k in zip(js, ks):
                    if 0 <= v - k < n_u:
                        acc[v - k] = acc[v - k] + taps[j] * window
        for u in range(n_u):
            acc_ref[cb, u * SUBLANES:(u + 1) * SUBLANES, :] = acc[u]
        return carry

    lax.fori_loop(0, n_cb, per_col_block, 0)
    h = jnp.concatenate([acc_ref[cb] for cb in range(n_cb)], axis=1)
    h = _layer_norm_rows(h, lg_ref[...], lb_ref[...])
    o_ref[0] = (h * jax.nn.sigmoid(h)).astype(BF16)


def _conv_module(z_lo, conv_w, conv_b, ln_g, ln_b):
    bsz, seq, _ = z_lo.shape
    bs = min(CONV_TILE, seq)
    vec = lambda v: v.reshape(1, D_GROUP)
    tile = lambda col: pl.BlockSpec((1, bs, D_GROUP), lambda b, i: (b, i, col))
    full = lambda r: pl.BlockSpec((r, D_GROUP), lambda b, i: (0, 0))
    n_cb = D_GROUP // LANES
    per_cb = lambda rows: pl.BlockSpec((n_cb, rows, LANES), lambda b, i: (0, 0, 0))
    w_cb = conv_w.reshape(CONV_WIDTH, n_cb, LANES).transpose(1, 0, 2)
    return pl.pallas_call(
        functools.partial(_conv_kernel, bs=bs),
        grid=(bsz, seq // bs),
        in_specs=[tile(ZLO_GLU_A // D_GROUP), tile(ZLO_GLU_B // D_GROUP),
                  per_cb(CONV_WIDTH), per_cb(1), full(1), full(1)],
        out_specs=pl.BlockSpec((1, bs, D_GROUP), lambda b, i: (b, i, 0)),
        out_shape=jax.ShapeDtypeStruct((bsz, seq, D_GROUP), BF16),
        scratch_shapes=[pltpu.VMEM((n_cb, CONV_HALO + bs, LANES), F32),
                        pltpu.VMEM((n_cb, bs, LANES), F32)],
        compiler_params=_cparams(("arbitrary", "arbitrary"), MIXER_VMEM),
        name="conv_module",
    )(z_lo, z_lo, w_cb, conv_b.reshape(n_cb, 1, LANES), vec(ln_g), vec(ln_b))


def _fox_kernel(q_ref, k_ref, v_ref, c3_ref, o_ref, ka_ref, va_ref, s0_ref, s1_ref, p_ref, m_ref, acc_ref,
                *, blk, sub, rg):
    h = pl.program_id(1)
    i = pl.program_id(2)

    @pl.when(i == 0)
    def _():
        ka_ref[:, :HEAD_DIM] = k_ref[0]
        ka_ref[:, HEAD_DIM:] = c3_ref[0]
        va_ref[:, :HEAD_DIM] = v_ref[0]
        lane = lax.broadcasted_iota(jnp.int32, (va_ref.shape[0], HEAD_DIM), 1)
        va_ref[:, HEAD_DIM:] = (lane == 0).astype(BF16)

    lane = lax.broadcasted_iota(jnp.int32, (blk, HEAD_DIM), 1)
    pick = (lane == h) | (lane == h + N_HEADS) | (lane == h + 2 * N_HEADS)
    qa = jnp.concatenate([q_ref[0], pick.astype(BF16)], axis=1)
    m_ref[...] = jnp.full(m_ref.shape, -jnp.inf, F32)
    acc_ref[...] = jnp.zeros(acc_ref.shape, F32)

    def logits_to(s_ref, jsub, r0=0):
        start = pl.multiple_of(jsub * sub, sub)
        s_ref[r0:, :] = lax.dot_general(qa[r0:], ka_ref[pl.ds(start, sub), :], (((1,), (1,)), ((), ())),
                                        preferred_element_type=F32)

    def update(s_ref, jsub, mask_off=None, r0=0):
        for r in range(r0 // rg, blk // rg):
            rows = slice(r * rg, (r + 1) * rg)
            s = s_ref[rows, :]
            if mask_off is not None and r * rg < mask_off + sub - 1:
                row = lax.broadcasted_iota(jnp.int32, (rg, sub), 0) + r * rg
                col = lax.broadcasted_iota(jnp.int32, (rg, sub), 1) + mask_off
                s = jnp.where(col <= row, s, -jnp.inf)
            m_old = m_ref[rows, :]
            m_new = jnp.maximum(m_old, jnp.max(s, axis=-1, keepdims=True))
            m_ref[rows, :] = m_new
            alpha = jnp.exp2(m_old - m_new)
            for cg in range(sub // LANES):
                cs = slice(cg * LANES, (cg + 1) * LANES)
                p_ref[rows, cs] = jnp.exp2(s[:, cs] - m_new).astype(BF16)
            for cg in range(acc_ref.shape[1] // LANES):
                cs = slice(cg * LANES, (cg + 1) * LANES)
                acc_ref[rows, cs] = alpha * acc_ref[rows, cs]
        start = pl.multiple_of(jsub * sub, sub)
        acc_ref[r0:, :] += jnp.dot(p_ref[r0:, :], va_ref[pl.ds(start, sub), :], preferred_element_type=F32)

    per = blk // sub
    logits_to(s0_ref, 0)

    def body(t, carry):
        logits_to(s1_ref, per * t + 1)
        update(s0_ref, per * t)
        logits_to(s0_ref, per * t + 2)
        update(s1_ref, per * t + 1)
        return carry

    lax.fori_loop(0, i, body, 0)
    logits_to(s1_ref, per * i + 1, r0=sub)
    update(s0_ref, per * i, mask_off=0)
    update(s1_ref, per * i + 1, mask_off=sub, r0=sub)
    acc = acc_ref[...]
    o_ref[0] = (acc[:, :HEAD_DIM] * (1.0 / acc[:, HEAD_DIM:HEAD_DIM + 1])).astype(BF16)


def _forgetting_attention(z_lo, c3):
    bsz, seq, _ = z_lo.shape
    blk = min(FOX_BLOCK, seq)
    sub = blk // 2
    rg = min(FOX_ROW_GROUP, blk)
    col = lambda off: off // HEAD_DIM
    kv = lambda off: pl.BlockSpec((1, seq, HEAD_DIM), lambda b, h, i: (b, 0, col(off) + h))
    return pl.pallas_call(
        functools.partial(_fox_kernel, blk=blk, sub=sub, rg=rg),
        grid=(bsz, N_HEADS, seq // blk),
        in_specs=[pl.BlockSpec((1, blk, HEAD_DIM), lambda b, h, i: (b, i, col(ZLO_FQ) + h)),
                  kv(ZLO_FK), kv(ZLO_FV),
                  pl.BlockSpec((1, seq, LANES), lambda b, h, i: (b, 0, 0))],
        out_specs=pl.BlockSpec((1, blk, HEAD_DIM), lambda b, h, i: (b, i, h)),
        out_shape=jax.ShapeDtypeStruct((bsz, seq, D_GROUP), BF16),
        scratch_shapes=[pltpu.VMEM((seq, 2 * HEAD_DIM), BF16), pltpu.VMEM((seq, 2 * HEAD_DIM), BF16),
                        pltpu.VMEM((blk, sub), F32), pltpu.VMEM((blk, sub), F32), pltpu.VMEM((blk, sub), BF16),
                        pltpu.VMEM((blk, LANES), F32),
                        pltpu.VMEM((blk, 2 * HEAD_DIM), F32)],
        compiler_params=_cparams(("arbitrary", "arbitrary", "arbitrary"), FOX_VMEM),
        name="forgetting_attention",
    )(z_lo, z_lo, z_lo, c3)


def _ret_kernel(q_ref, k_ref, v_ref, g_ref, cos_ref, sin_ref, dec_ref, qw_ref, kw_ref, cd_ref, bd_ref,
                o_ref, state_ref, *, chunk):
    n = pl.program_id(1)

    @pl.when(n == 0)
    def _():
        state_ref[...] = jnp.zeros_like(state_ref)

    cos = cos_ref[...]
    sin = sin_ref[...]
    lane = lax.broadcasted_iota(jnp.int32, (chunk, LANES), 1)
    first_half = (lane & (RET_K_DIM // 2)) == 0
    low_head = lane < RET_K_DIM

    def rotate(x):
        swapped = jnp.where(first_half, pltpu.roll(x, LANES - RET_K_DIM // 2, 1), pltpu.roll(x, RET_K_DIM // 2, 1))
        return x * cos + swapped * sin

    for pr in range(N_HEADS // 2):
        ls = slice(pr * LANES, (pr + 1) * LANES)
        qr = rotate(q_ref[0, :, ls].astype(F32))
        kr = rotate(k_ref[0, :, ls].astype(F32))
        kb = kr.astype(BF16)
        v_pair = v_ref[0, :, pr * 2 * HEAD_DIM:(pr + 1) * 2 * HEAD_DIM]
        state = state_ref[pr]
        cross = jnp.dot((qr * qw_ref[:, ls]).astype(BF16), state.astype(BF16), preferred_element_type=F32)
        for a in range(2):
            head = 2 * pr + a
            hs = slice(head * HEAD_DIM, (head + 1) * HEAD_DIM)
            qa = jnp.where(low_head if a == 0 else jnp.logical_not(low_head), qr, 0.0).astype(BF16)
            scores = lax.dot_general(qa, kb, (((1,), (1,)), ((), ())), preferred_element_type=F32) * dec_ref[head]
            inner = jnp.dot(scores.astype(BF16), v_pair[:, a * HEAD_DIM:(a + 1) * HEAD_DIM],
                            preferred_element_type=F32)
            y = inner + cross[:, a * HEAD_DIM:(a + 1) * HEAD_DIM]
            mu = jnp.mean(y, axis=-1, keepdims=True)
            yc = y - mu
            var = jnp.mean(yc * yc, axis=-1, keepdims=True)
            yn = yc * lax.rsqrt(var + GN_EPS)
            gate = g_ref[0, :, hs].astype(F32)
            o_ref[0, :, hs] = (gate * jax.nn.sigmoid(gate) * yn).astype(BF16)
        kw_t = (kr * kw_ref[:, ls]).T.astype(BF16)
        kv = jnp.dot(kw_t, v_pair, preferred_element_type=F32)
        state_ref[pr] = state * cd_ref[pr] + kv * bd_ref[...]


def _retention_tables(seq, chunk):
    half = RET_K_DIM // 2
    inv = 1.0 / (ROPE_BASE ** jnp.linspace(0.0, 1.0, half, dtype=F32))
    ang = jnp.arange(seq).astype(F32)[:, None] * inv[None, :]
    cos = jnp.tile(jnp.cos(ang), (1, LANES // half))
    sin = jnp.tile(jnp.concatenate([-jnp.sin(ang), jnp.sin(ang)], axis=-1), (1, LANES // RET_K_DIM))
    log_gamma = jnp.log(1.0 - 2.0 ** (-5.0 - jnp.arange(N_HEADS, dtype=F32)))
    idx = jnp.arange(chunk, dtype=F32)
    rel = idx[:, None] - idx[None, :]
    decay = jnp.where(rel >= 0, jnp.exp(log_gamma[:, None, None] * jnp.maximum(rel, 0.0)), 0.0)
    k_w = jnp.exp(log_gamma[:, None] * (chunk - 1.0 - idx)[None, :])
    q_w = jnp.exp(log_gamma[:, None] * (idx + 1.0)[None, :])
    per_lane = lambda t: jnp.repeat(t.T, RET_K_DIM, axis=1)
    chunk_decay = jnp.exp(log_gamma * chunk)
    cd = jnp.broadcast_to(jnp.repeat(chunk_decay, RET_K_DIM).reshape(N_HEADS // 2, LANES, 1),
                          (N_HEADS // 2, LANES, 2 * HEAD_DIM))
    r = jnp.arange(LANES)[:, None] // RET_K_DIM
    c = jnp.arange(2 * HEAD_DIM)[None, :] // HEAD_DIM
    bd = (r == c).astype(F32)
    k_scale = RET_K_DIM ** -0.5
    return cos, sin, decay * k_scale, per_lane(q_w), per_lane(k_w) * k_scale, cd, bd


def _retention(z_hi):
    bsz, seq, _ = z_hi.shape
    chunk = min(RET_CHUNK, seq)
    cos, sin, decay, q_w, k_w, cd, bd = _retention_tables(seq, chunk)
    d_qk = N_HEADS * RET_K_DIM
    const = lambda shape: pl.BlockSpec(shape, lambda b, n: (0,) * len(shape))
    return pl.pallas_call(
        functools.partial(_ret_kernel, chunk=chunk),
        grid=(bsz, seq // chunk),
        in_specs=[pl.BlockSpec((1, chunk, d_qk), lambda b, n: (b, n, ZHI_RQ // d_qk)),
                  pl.BlockSpec((1, chunk, d_qk), lambda b, n: (b, n, ZHI_RK // d_qk)),
                  pl.BlockSpec((1, chunk, D_GROUP), lambda b, n: (b, n, ZHI_RV // D_GROUP)),
                  pl.BlockSpec((1, chunk, D_GROUP), lambda b, n: (b, n, ZHI_RG // D_GROUP)),
                  pl.BlockSpec((chunk, LANES), lambda b, n: (n, 0)),
                  pl.BlockSpec((chunk, LANES), lambda b, n: (n, 0)),
                  const((N_HEADS, chunk, chunk)), const((chunk, d_qk)), const((chunk, d_qk)),
                  const((N_HEADS // 2, LANES, 2 * HEAD_DIM)), const((LANES, 2 * HEAD_DIM))],
        out_specs=pl.BlockSpec((1, chunk, D_GROUP), lambda b, n: (b, n, 0)),
        out_shape=jax.ShapeDtypeStruct((bsz, seq, D_GROUP), BF16),
        scratch_shapes=[pltpu.VMEM((N_HEADS // 2, LANES, 2 * HEAD_DIM), F32)],
        compiler_params=_cparams(("arbitrary", "arbitrary"), MIXER_VMEM),
        name="retention",
    )(z_hi, z_hi, z_hi, z_hi, cos, sin, decay, q_w, k_w, cd, bd)


def _gelu_tanh(x):
    return 0.5 * x * (1.0 + jnp.tanh(math.sqrt(2.0 / math.pi) * (x + 0.044715 * (x * x * x))))


def _sg_kernel(u_ref, v_ref, w_ref, bt_ref, lg_ref, lb_ref, o_ref, *, bs):
    u = _gelu_tanh(u_ref[0].astype(F32))
    v = _layer_norm_rows(_gelu_tanh(v_ref[0].astype(F32)), lg_ref[...], lb_ref[...]).astype(BF16)
    row = lax.broadcasted_iota(jnp.int32, (SG_CHUNK, SG_CHUNK), 0)
    col = lax.broadcasted_iota(jnp.int32, (SG_CHUNK, SG_CHUNK), 1)
    causal = col <= row
    for g in range(N_HEADS):
        cs = slice(g * HEAD_DIM, (g + 1) * HEAD_DIM)
        w_g = jnp.where(causal, w_ref[g], 0.0).astype(BF16)
        bias = bt_ref[:, g:g + 1]
        for r in range(bs // SG_CHUNK):
            rs = slice(r * SG_CHUNK, (r + 1) * SG_CHUNK)
            mixed = jnp.dot(w_g, v[rs, cs], preferred_element_type=F32) + bias
            o_ref[0, rs, cs] = (u[rs, cs] * mixed).astype(BF16)


def _spatial_gating(z_hi, sg_w, sg_b, ln_g, ln_b):
    bsz, seq, _ = z_hi.shape
    bs = min(SG_TILE, seq)
    vec = lambda t: t.reshape(1, D_GROUP)
    tile = lambda off: pl.BlockSpec((1, bs, D_GROUP), lambda b, i: (b, i, off // D_GROUP))
    return pl.pallas_call(
        functools.partial(_sg_kernel, bs=bs),
        grid=(bsz, seq // bs),
        in_specs=[tile(ZHI_SGU), tile(ZHI_SGV),
                  pl.BlockSpec((N_HEADS, SG_CHUNK, SG_CHUNK), lambda b, i: (0, 0, 0)),
                  pl.BlockSpec((SG_CHUNK, N_HEADS), lambda b, i: (0, 0)),
                  pl.BlockSpec((1, D_GROUP), lambda b, i: (0, 0)),
                  pl.BlockSpec((1, D_GROUP), lambda b, i: (0, 0))],
        out_specs=pl.BlockSpec((1, bs, D_GROUP), lambda b, i: (b, i, 0)),
        out_shape=jax.ShapeDtypeStruct((bsz, seq, D_GROUP), BF16),
        compiler_params=_cparams(("arbitrary", "arbitrary"), MIXER_VMEM),
        name="spatial_gating",
    )(z_hi, z_hi, sg_w, sg_b.T, vec(ln_g), vec(ln_b))


def _forget_weights(w_gate_l, f_bias_l):
    d = w_gate_l.shape[0]
    wf = jnp.concatenate([w_gate_l, w_gate_l, w_gate_l, jnp.zeros((d, LANES - 3 * N_HEADS), F32)], axis=1).astype(BF16)
    fb = jnp.concatenate([f_bias_l, f_bias_l, f_bias_l, jnp.zeros((LANES - 3 * N_HEADS,), F32)]).reshape(1, LANES)
    return wf, fb


def kernel(x, c, ada_w, ada_b, mix_pre_g, mix_post_g, w_in, fox_f_bias, conv_w, conv_b, conv_ln_g, conv_ln_b,
           sg_w, sg_b, sg_ln_g, sg_ln_b, w_out, ffn_pre_g, ffn_post_g, w_ff1, w_ff2):
    bsz, seq, d = x.shape
    n_layers = ada_w.shape[0]
    m = bsz * seq
    mod = _modulation(c, ada_w, ada_b).reshape(n_layers, bsz, 6, 1, d)
    shift1, scale1, gate1, shift2, scale2, gate2 = [mod[:, :, t] for t in range(6)]

    w_in_t = jnp.transpose(w_in, (0, 2, 1))
    q_scale = jnp.ones((FF_OFF, 1), F32).at[ZLO_FQ:ZLO_FQ + D_GROUP].set(FOX_SCALE)
    w_lo_b = (w_in_t[:, :FF_OFF, :] * q_scale).astype(BF16)
    w_hi_b = w_in_t[:, FF_OFF + N_HEADS:, :].astype(BF16)
    w_gate = w_in[:, :, FF_OFF:FF_OFF + N_HEADS]

    x2 = x.reshape(m, d)
    h, c3 = _rows(x2, seq, pre=(mix_pre_g[0], scale1[0], shift1[0]),
                  forget=_forget_weights(w_gate[0], fox_f_bias[0]))
    for l in range(n_layers):
        z_lo, w_out_b = _matmul([h], w_lo_b, l, BF16, side=(w_out, l), w_t=True)
        z_hi, w_ff1_b = _matmul([h], w_hi_b, l, BF16, side=(w_ff1, l), w_t=True)
        z_lo = z_lo.reshape(bsz, seq, D_ZHALF)
        z_hi = z_hi.reshape(bsz, seq, D_ZHALF)
        y_conv = _conv_module(z_lo, conv_w[l], conv_b[l], conv_ln_g[l], conv_ln_b[l])
        y_fox = _forgetting_attention(z_lo, c3)
        y_ret = _retention(z_hi)
        y_sg = _spatial_gating(z_hi, sg_w[l], sg_b[l], sg_ln_g[l], sg_ln_b[l])
        parts = [t.reshape(m, D_GROUP) for t in (y_conv, y_fox, y_ret, y_sg)]
        y = _matmul(parts, w_out_b, 0, BF16)
        x2, h = _rows(x2, seq, post=(y, mix_post_g[l], gate1[l]), pre=(ffn_pre_g[l], scale2[l], shift2[l]))
        u, w_ff2_b = _matmul([h], w_ff1_b, 0, BF16, relu2=True, side=(w_ff2, l))
        y = _matmul_ksplit(u, w_ff2_b, 0, BF16)
        if l + 1 < n_layers:
            x2, h, c3 = _rows(x2, seq, post=(y, ffn_post_g[l], gate2[l]),
                              pre=(mix_pre_g[l + 1], scale1[l + 1], shift1[l + 1]),
                              forget=_forget_weights(w_gate[l + 1], fox_f_bias[l + 1]))
        else:
            (x2,) = _rows(x2, seq, post=(y, ffn_post_g[l], gate2[l]))
    return x2.reshape(bsz, seq, d)
```
